```python
import math
import jax, jax.numpy as jnp
from jax import lax
import numpy as np

D_MODEL = 1024
BATCH = 8
SEQ = 4096
DEPTH = 1

GDN_HEADS = 8
GDN_DK = 128
GDN_DV = 128
GDN_CONV = 4
GDN_CHUNK = 64
DSA_HEADS = 8
DSA_KV_HEADS = 2
DSA_DH = 128
IDX_HEADS = 8
IDX_DIM = 64
TOPK_MAX = 256
Q_BLOCK = 128
D_FF = 2816
FFN_CONV = 3
EPS = 1e-6

GDN_QK_W = GDN_HEADS * GDN_DK
GDN_V_W = GDN_HEADS * GDN_DV
GDN_CONV_CH = 2 * GDN_QK_W + GDN_V_W
DSA_Q_W = DSA_HEADS * DSA_DH
DSA_KV_W = DSA_KV_HEADS * DSA_DH
IDX_Q_W = IDX_HEADS * IDX_DIM
IN_SPLITS = (GDN_CONV_CH, GDN_V_W, GDN_HEADS, GDN_HEADS,
             DSA_Q_W, DSA_KV_W, DSA_KV_W, IDX_Q_W, IDX_DIM, IDX_HEADS,
             D_MODEL, D_MODEL)
N_IN = GDN_CONV_CH + GDN_V_W + 2 * GDN_HEADS + DSA_Q_W + 2 * DSA_KV_W + IDX_Q_W + IDX_DIM + IDX_HEADS + 2 * D_MODEL

kernel_name = "hybrid_gdn_dsa_convglu_block"


def rmsnorm(x, g):
    xf = x.astype(jnp.float32)
    y = xf * lax.rsqrt(jnp.mean(xf * xf, axis=-1, keepdims=True) + EPS)
    return (y * g.astype(jnp.float32)).astype(x.dtype)


def layernorm_nobias(x, g):
    xf = x.astype(jnp.float32)
    xc = xf - jnp.mean(xf, axis=-1, keepdims=True)
    y = xc * lax.rsqrt(jnp.mean(xc * xc, axis=-1, keepdims=True) + EPS)
    return (y * g.astype(jnp.float32)).astype(x.dtype)


def l2norm(x):
    xf = x.astype(jnp.float32)
    return xf * lax.rsqrt(jnp.sum(xf * xf, axis=-1, keepdims=True) + EPS)


def causal_dwconv(x, w):
    width, ch = w.shape
    return lax.conv_general_dilated(
        x, w[:, None, :].astype(x.dtype), window_strides=(1,),
        padding=[(width - 1, 0)], dimension_numbers=("NWC", "WIO", "NWC"),
        feature_group_count=ch)


def gated_delta_rule_chunked(q, k, v, g, beta):
    B, S, H, dk = q.shape
    dv = v.shape[-1]
    C = GDN_CHUNK
    N = S // C
    q = q * (dk ** -0.5)

    def chunks(t):
        return jnp.swapaxes(t.reshape((B, N, C) + t.shape[2:]), 2, 3)

    qc, kc, vc, gc, bc = chunks(q), chunks(k), chunks(v), chunks(g), chunks(beta)
    G = jnp.cumsum(gc, axis=-1)
    tri = jnp.tril(jnp.ones((C, C), dtype=bool))
    strict = jnp.tril(jnp.ones((C, C), dtype=bool), -1)
    gamma = jnp.exp(jnp.where(tri, G[..., :, None] - G[..., None, :], -jnp.inf))
    kb = kc * bc[..., None]
    A = jnp.where(strict, jnp.einsum("bnhid,bnhjd->bnhij", kb, kc) * gamma, 0.0)
    eye = jnp.eye(C, dtype=jnp.float32)
    T = lax.linalg.triangular_solve(A + eye, jnp.broadcast_to(eye, A.shape),
                                    left_side=True, lower=True)
    eG = jnp.exp(G)[..., None]
    u = jnp.einsum("bnhij,bnhje->bnhie", T, vc * bc[..., None])
    w = jnp.einsum("bnhij,bnhjd->bnhid", T, kb * eG)
    qg = qc * eG
    kd = kc * jnp.exp(G[..., -1:] - G)[..., None]
    att = jnp.einsum("bnhid,bnhjd->bnhij", qc, kc) * gamma
    dlast = jnp.exp(G[..., -1])

    def step(state, inp):
        u_i, w_i, qg_i, kd_i, att_i, dl_i = inp
        v_new = u_i - jnp.einsum("bhcd,bhde->bhce", w_i, state)
        o = jnp.einsum("bhcd,bhde->bhce", qg_i, state) + jnp.einsum("bhij,bhje->bhie", att_i, v_new)
        state = state * dl_i[..., None, None] + jnp.einsum("bhcd,bhce->bhde", kd_i, v_new)
        return state, o

    xs = (jnp.swapaxes(u, 0, 1), jnp.swapaxes(w, 0, 1), jnp.swapaxes(qg, 0, 1),
          jnp.swapaxes(kd, 0, 1), jnp.swapaxes(att, 0, 1), jnp.swapaxes(dlast, 0, 1))
    s0 = jnp.zeros((B, H, dk, dv), jnp.float32)
    _, o = lax.scan(step, s0, xs)
    return jnp.transpose(o, (1, 0, 3, 2, 4)).reshape(B, S, H, dv)


def dsa_sparse_attention(q, k, v, iq, ik, iw):
    B, S, Hq, dh = q.shape
    L, Hkv = k.shape[1], k.shape[2]
    k_top = min(TOPK_MAX, L // 4)
    rep = Hq // Hkv
    nb = S // Q_BLOCK
    key_pos = jnp.arange(L, dtype=jnp.int32)
    starts = jnp.arange(nb, dtype=jnp.int32) * Q_BLOCK
    iw = iw * ((IDX_HEADS ** -0.5) * (IDX_DIM ** -0.5))

    def blocks(t):
        return jnp.swapaxes(t.reshape((B, nb, Q_BLOCK) + t.shape[2:]), 0, 1)

    def one_block(args):
        q_b, iq_b, iw_b, start = args
        t_pos = start + jnp.arange(Q_BLOCK, dtype=jnp.int32)
        causal = key_pos[None, :] <= t_pos[:, None]
        rel = jax.nn.relu(jnp.einsum("bqhd,bsd->bqhs", iq_b, ik))
        score = jnp.einsum("bqh,bqhs->bqs", iw_b, rel).astype(jnp.float32)
        score = jnp.where(causal[None], score, -jnp.inf)
        _, idx = lax.top_k(score, k_top)
        valid = idx <= t_pos[None, :, None]
        k_sel = jax.vmap(lambda kb, ib: kb[ib])(k, idx)
        v_sel = jax.vmap(lambda vb, ib: vb[ib])(v, idx)
        qg = q_b.reshape(B, Q_BLOCK, Hkv, rep, dh)
        logits = jnp.einsum("bqgrd,bqkgd->bqgrk", qg, k_sel).astype(jnp.float32) * (dh ** -0.5)
        logits = jnp.where(valid[:, :, None, None, :], logits, -jnp.inf)
        p = jax.nn.softmax(logits, axis=-1).astype(v.dtype)
        o = jnp.einsum("bqgrk,bqkgd->bqgrd", p, v_sel)
        return o.reshape(B, Q_BLOCK, Hq * dh)

    out = lax.map(one_block, (blocks(q), blocks(iq), blocks(iw), starts))
    return jnp.swapaxes(out, 0, 1).reshape(B, S, Hq * dh)


def setup_inputs(seed: int = 0) -> dict:
    key = jax.random.key(seed)
    ks = jax.random.split(key, 16)
    f32 = jnp.float32

    def nrm(k, shape, scale):
        return jax.random.normal(k, shape, f32) * scale

    x = nrm(ks[0], (BATCH, SEQ, D_MODEL), 1.0)
    norm1_g = 1.0 + nrm(ks[1], (DEPTH, D_MODEL), 0.02)
    w_in = nrm(ks[2], (DEPTH, D_MODEL, N_IN), D_MODEL ** -0.5)
    gdn_conv_w = nrm(ks[3], (DEPTH, GDN_CONV, GDN_CONV_CH), GDN_CONV ** -0.5)
    gdn_a_log = jnp.log(jax.random.uniform(ks[4], (DEPTH, GDN_HEADS), f32, 1.0, 16.0))
    dt = jnp.exp(jax.random.uniform(ks[5], (DEPTH, GDN_HEADS), f32, math.log(1e-3), math.log(1e-1)))
    gdn_dt_bias = dt + jnp.log(-jnp.expm1(-dt))
    gdn_norm_g = 1.0 + nrm(ks[6], (DEPTH, GDN_DV), 0.02)
    idx_k_norm_g = 1.0 + nrm(ks[7], (DEPTH, IDX_DIM), 0.02)
    branch_gate_b = nrm(ks[8], (DEPTH, 2 * D_MODEL), 0.02)
    w_out = nrm(ks[9], (DEPTH, D_MODEL, D_MODEL), D_MODEL ** -0.5)
    norm2_g = 1.0 + nrm(ks[10], (DEPTH, D_MODEL), 0.02)
    ffn_w_up = nrm(ks[11], (DEPTH, D_MODEL, 2 * D_FF), D_MODEL ** -0.5)
    ffn_conv_w = nrm(ks[12], (DEPTH, FFN_CONV, 2 * D_FF), FFN_CONV ** -0.5)
    ffn_conv_b = nrm(ks[13], (DEPTH, 2 * D_FF), 0.02)
    ffn_w_down = nrm(ks[14], (DEPTH, D_FF, D_MODEL), D_FF ** -0.5)
    final_g = 1.0 + nrm(ks[15], (D_MODEL,), 0.02)
    return {"x": x, "norm1_g": norm1_g, "w_in": w_in, "gdn_conv_w": gdn_conv_w,
            "gdn_a_log": gdn_a_log, "gdn_dt_bias": gdn_dt_bias, "gdn_norm_g": gdn_norm_g,
            "idx_k_norm_g": idx_k_norm_g, "branch_gate_b": branch_gate_b, "w_out": w_out,
            "norm2_g": norm2_g, "ffn_w_up": ffn_w_up, "ffn_conv_w": ffn_conv_w,
            "ffn_conv_b": ffn_conv_b, "ffn_w_down": ffn_w_down, "final_g": final_g}


def reference(x, norm1_g, w_in, gdn_conv_w, gdn_a_log, gdn_dt_bias, gdn_norm_g,
              idx_k_norm_g, branch_gate_b, w_out, norm2_g, ffn_w_up, ffn_conv_w,
              ffn_conv_b, ffn_w_down, final_g):
    B, S, _ = x.shape
    offsets = [int(o) for o in np.cumsum(IN_SPLITS)[:-1]]
    h = x
    for l in range(DEPTH):
        u = rmsnorm(h, norm1_g[l])
        proj = jnp.einsum("bsd,de->bse", u, w_in[l])
        (g_qkv, g_z, g_a, g_b, d_q, d_k, d_v, i_q, i_k, i_w, gate_a, gate_b) = jnp.split(proj, offsets, axis=-1)

        qkv = jax.nn.silu(causal_dwconv(g_qkv, gdn_conv_w[l]))
        gq, gk, gv = jnp.split(qkv, [GDN_QK_W, 2 * GDN_QK_W], axis=-1)
        gq = l2norm(gq.reshape(B, S, GDN_HEADS, GDN_DK))
        gk = l2norm(gk.reshape(B, S, GDN_HEADS, GDN_DK))
        gv = gv.reshape(B, S, GDN_HEADS, GDN_DV).astype(jnp.float32)
        beta = jax.nn.sigmoid(g_b.astype(jnp.float32))
        logdecay = -jnp.exp(gdn_a_log[l].astype(jnp.float32)) * jax.nn.softplus(
            g_a.astype(jnp.float32) + gdn_dt_bias[l].astype(jnp.float32))
        o_gdn = gated_delta_rule_chunked(gq, gk, gv, logdecay, beta)
        z = g_z.reshape(B, S, GDN_HEADS, GDN_DV).astype(jnp.float32)
        o_gdn = (rmsnorm(o_gdn, gdn_norm_g[l]) * jax.nn.silu(z)).reshape(B, S, GDN_V_W).astype(x.dtype)

        o_dsa = dsa_sparse_attention(
            d_q.reshape(B, S, DSA_HEADS, DSA_DH),
            d_k.reshape(B, S, DSA_KV_HEADS, DSA_DH),
            d_v.reshape(B, S, DSA_KV_HEADS, DSA_DH),
            i_q.reshape(B, S, IDX_HEADS, IDX_DIM),
            layernorm_nobias(i_k, idx_k_norm_g[l]),
            i_w)

        gb = branch_gate_b[l]
        mixed = (jax.nn.sigmoid(gate_a + gb[:D_MODEL]) * o_gdn
                 + jax.nn.sigmoid(gate_b + gb[D_MODEL:]) * o_dsa)
        h = h + jnp.einsum("bse,ed->bsd", mixed, w_out[l])

        hn = rmsnorm(h, norm2_g[l])
        up = causal_dwconv(jnp.einsum("bsd,df->bsf", hn, ffn_w_up[l]), ffn_conv_w[l]) + ffn_conv_b[l]
        f_gate, f_val = jnp.split(up, [D_FF], axis=-1)
        h = h + jnp.einsum("bsf,fd->bsd", jax.nn.silu(f_gate) * f_val, ffn_w_down[l])
    return rmsnorm(h, final_g)
```

```python
import functools

import jax
import jax.numpy as jnp
import numpy as np
from jax import lax
from jax.experimental import pallas as pl
from jax.experimental.pallas import tpu as pltpu

F32 = jnp.float32
BF16 = jnp.bfloat16
I32 = jnp.int32

LANES = 128
SUBLANES = 8
VMEM_LIMIT = 56 * 1024 * 1024

D_MODEL = 1024
GDN_HEADS = 8
GDN_D = 128
GDN_CONV = 4
DSA_HEADS = 8
DSA_KV_HEADS = 2
DSA_DH = 128
IDX_HEADS = 8
IDX_DIM = 64
TOPK_MAX = 256
D_FF = 2816
FFN_CONV = 3
EPS = 1e-6

COL_QKV = 0
COL_Z = 3072
COL_DQ = 4096
COL_DK = 5120
COL_DV = 5376
COL_IQ = 5632
COL_GA = 6144
COL_GB = 7168
N_BIG = 8192
SM_IK = 0
SM_A = 64
SM_B = 72
SM_IW = 80

NEG_BIG = -1e30


def _cparams(sem):
    return pltpu.CompilerParams(dimension_semantics=sem, vmem_limit_bytes=VMEM_LIMIT)


def _dot(a, b):
    return jnp.dot(a, b, preferred_element_type=F32)


def _dot_nt(a, b):
    return lax.dot_general(a, b, (((1,), (1,)), ((), ())), preferred_element_type=F32)


def _dot_tn(a, b):
    return lax.dot_general(a, b, (((0,), (0,)), ((), ())), preferred_element_type=F32)


def _split3(x):
    h = x.astype(BF16)
    r = x - h.astype(F32)
    m = r.astype(BF16)
    l = (r - m.astype(F32)).astype(BF16)
    return h, m, l


def _split2(x):
    h = x.astype(BF16)
    l = (x - h.astype(F32)).astype(BF16)
    return h, l


def _mm3(a2, b2):
    ah, al = a2
    bh, bl = b2
    return _dot(ah, bh) + (_dot(ah, bl) + _dot(al, bh))


def _sigmoid(x):
    return 1.0 / (1.0 + jnp.exp(-x))


def _silu(x):
    return x * _sigmoid(x)


def _softplus(x):
    return jnp.maximum(x, 0.0) + jnp.log(1.0 + jnp.exp(-jnp.abs(x)))


def _inproj_body(x_ref, g_ref, w_ref, ws_ref, proj_ref, small_ref, *, col_chunk):
    x = x_ref[...]
    ms = jnp.mean(x * x, axis=-1, keepdims=True)
    u = ((x * lax.rsqrt(ms + EPS)) * g_ref[...]).astype(BF16)
    for c in range(N_BIG // col_chunk):
        sl = slice(c * col_chunk, (c + 1) * col_chunk)
        proj_ref[:, sl] = _dot(u, w_ref[:, sl]).astype(BF16)
    small_ref[...] = _dot(u, ws_ref[...])


def _inproj(x2, g, w_big, w_small, *, tm=256):
    n = x2.shape[0]
    const = lambda i: (0, 0)
    return pl.pallas_call(
        functools.partial(_inproj_body, col_chunk=1024),
        grid=(n // tm,),
        in_specs=[
            pl.BlockSpec((tm, D_MODEL), lambda i: (i, 0)),
            pl.BlockSpec((1, D_MODEL), const),
            pl.BlockSpec((D_MODEL, N_BIG), const, pipeline_mode=pl.Buffered(1)),
            pl.BlockSpec((D_MODEL, LANES), const, pipeline_mode=pl.Buffered(1)),
        ],
        out_specs=[
            pl.BlockSpec((tm, N_BIG), lambda i: (i, 0)),
            pl.BlockSpec((tm, LANES), lambda i: (i, 0)),
        ],
        out_shape=[
            jax.ShapeDtypeStruct((n, N_BIG), BF16),
            jax.ShapeDtypeStruct((n, LANES), F32),
        ],
        compiler_params=_cparams(("arbitrary",)),
        name="inproj",
    )(x2, g, w_big, w_small)


GDN_C = 128


def _tri_inverse(a):
    n = a.shape[0]
    row = lax.broadcasted_iota(I32, (n, n), 0)
    col = lax.broadcasted_iota(I32, (n, n), 1)
    eye = jnp.where(row == col, 1.0, 0.0).astype(F32)

    def same_block(size):
        sh = size.bit_length() - 1
        return (row >> sh) == (col >> sh)

    p1 = jnp.where(same_block(16), a, 0.0)
    p1s = _split2(p1)
    p2 = _mm3(p1s, p1s)
    p2s = _split2(p2)
    p4 = _mm3(p2s, p2s)
    p4s = _split2(p4)
    p8 = _mm3(p4s, p4s)
    p8s = _split2(p8)
    t = eye - p1
    t = t + _mm3(_split2(t), p2s)
    t = t + _mm3(_split2(t), p4s)
    t = t + _mm3(_split2(t), p8s)
    size = 16
    while size < n:
        off = jnp.where(same_block(2 * size) & jnp.logical_not(same_block(size)), a, 0.0)
        ts = _split2(t)
        t = t - _mm3(_split2(_mm3(ts, _split2(off))), ts)
        size *= 2
    return t


def _gdn_body(qp_ref, kp_ref, vp_ref, z_ref, sm_ref, wq_ref, wk_ref, wv_ref,
              alog_ref, dtb_ref, ng_ref, o_ref,
              state_ref, halo_ref, xx_ref, *, tb):
    h = pl.program_id(1)
    t = pl.program_id(2)

    @pl.when(t == 0)
    def _():
        state_ref[...] = jnp.zeros_like(state_ref)
        halo_ref[...] = jnp.zeros_like(halo_ref)

    def conv_silu(idx, x_ref, w_ref):
        x = x_ref[...].astype(F32)
        xx_ref[0:SUBLANES, :] = halo_ref[idx]
        xx_ref[SUBLANES:SUBLANES + tb, :] = x
        halo_ref[idx] = x[tb - SUBLANES:tb, :]
        w = w_ref[...]
        y = w[GDN_CONV - 1:GDN_CONV, :] * x
        for j in range(GDN_CONV - 1):
            s0 = SUBLANES - (GDN_CONV - 1) + j
            y = y + w[j:j + 1, :] * xx_ref[s0:s0 + tb, :]
        return _silu(y)

    q = conv_silu(0, qp_ref, wq_ref)
    k = conv_silu(1, kp_ref, wk_ref)
    v = conv_silu(2, vp_ref, wv_ref)
    q = q * lax.rsqrt(jnp.sum(q * q, axis=-1, keepdims=True) + EPS) * (GDN_D ** -0.5)
    k = k * lax.rsqrt(jnp.sum(k * k, axis=-1, keepdims=True) + EPS)

    sm = sm_ref[...]
    lane = lax.broadcasted_iota(I32, sm.shape, 1)
    ld_all = -jnp.exp(alog_ref[...]) * _softplus(sm + dtb_ref[...])
    ld = jnp.sum(jnp.where(lane == SM_A + h, ld_all, 0.0), axis=-1, keepdims=True)
    beta = jnp.sum(jnp.where(lane == SM_B + h, _sigmoid(sm), 0.0), axis=-1, keepdims=True)

    c = GDN_C
    row = lax.broadcasted_iota(I32, (c, c), 0)
    col = lax.broadcasted_iota(I32, (c, c), 1)
    tri_incl = jnp.where(row >= col, 1.0, 0.0).astype(BF16)
    z = z_ref[...].astype(F32)
    ng = ng_ref[...]

    for ci in range(tb // c):
        rs = slice(ci * c, (ci + 1) * c)
        qc, kc, vc = q[rs], k[rs], v[rs]
        bc = beta[rs]
        l1, l2, l3 = _split3(jnp.broadcast_to(ld[rs], (c, c)))
        gc = _dot(tri_incl, l1) + (_dot(tri_incl, l2) + _dot(tri_incl, l3))
        gr = gc.T
        gamma = jnp.exp(jnp.where(row >= col, gc - gr, -jnp.inf))
        eg = jnp.exp(gc)
        g_last = gc[c - 1:c, :]
        kb = kc * bc
        kc16 = kc.astype(BF16)
        a = jnp.where(row > col, _dot_nt(kb.astype(BF16), kc16) * gamma, 0.0)
        att = _dot_nt(qc.astype(BF16), kc16) * gamma
        tinv = _tri_inverse(a).astype(BF16)
        u = _dot(tinv, (vc * bc).astype(BF16))
        w = _dot(tinv, (kb * eg).astype(BF16))
        qg = (qc * eg).astype(BF16)
        kd = (kc * jnp.exp(g_last - gc)).astype(BF16)

        s_old = state_ref[...]
        s16 = s_old.astype(BF16)
        v_new = u - _dot(w.astype(BF16), s16)
        v_new16 = v_new.astype(BF16)
        o = _dot(qg, s16) + _dot(att.astype(BF16), v_new16)
        state_ref[...] = s_old * jnp.exp(g_last) + _dot_tn(kd, v_new16)

        y = o * lax.rsqrt(jnp.mean(o * o, axis=-1, keepdims=True) + EPS) * ng
        o_ref[rs, :] = (y * _silu(z[rs])).astype(o_ref.dtype)


def _gdn(proj, small, conv_w, alog_v, dtb_v, ng, *, tb=256):
    b, s, _ = proj.shape
    hq = COL_QKV // LANES
    hz = COL_Z // LANES
    nh = GDN_HEADS
    return pl.pallas_call(
        functools.partial(_gdn_body, tb=tb),
        grid=(b, nh, s // tb),
        in_specs=[
            pl.BlockSpec((None, tb, LANES), lambda bi, h, t: (bi, t, hq + h)),
            pl.BlockSpec((None, tb, LANES), lambda bi, h, t: (bi, t, hq + nh + h)),
            pl.BlockSpec((None, tb, LANES), lambda bi, h, t: (bi, t, hq + 2 * nh + h)),
            pl.BlockSpec((None, tb, LANES), lambda bi, h, t: (bi, t, hz + h)),
            pl.BlockSpec((None, tb, LANES), lambda bi, h, t: (bi, t, 0)),
            pl.BlockSpec((GDN_CONV, LANES), lambda bi, h, t: (0, h)),
            pl.BlockSpec((GDN_CONV, LANES), lambda bi, h, t: (0, nh + h)),
            pl.BlockSpec((GDN_CONV, LANES), lambda bi, h, t: (0, 2 * nh + h)),
            pl.BlockSpec((1, LANES), lambda bi, h, t: (0, 0)),
            pl.BlockSpec((1, LANES), lambda bi, h, t: (0, 0)),
            pl.BlockSpec((1, LANES), lambda bi, h, t: (0, 0)),
        ],
        out_specs=pl.BlockSpec((None, tb, LANES), lambda bi, h, t: (bi, t, h)),
        out_shape=jax.ShapeDtypeStruct((b, s, nh * GDN_D), BF16),
        scratch_shapes=[
            pltpu.VMEM((GDN_D, GDN_D), F32),
            pltpu.VMEM((3, SUBLANES, LANES), F32),
            pltpu.VMEM((SUBLANES + tb, LANES), F32),
        ],
        compiler_params=_cparams(("arbitrary", "arbitrary", "arbitrary")),
        name="gdn",
    )(proj, proj, proj, proj, small, conv_w, conv_w, conv_w, alog_v, dtb_v, ng)


DSA_QB = 128
DSA_KC = 512

_KEY_NEG_INF = np.int32(np.array(0xFF800000, np.uint32).view(np.int32) ^ np.int32(0x7FFFFFFF))
_KEY_POS_INF_P1 = np.int32(0x7F800001)


def _key_to_f32(key):
    bits = key ^ ((key >> 31) & jnp.int32(0x7FFFFFFF))
    return lax.bitcast_convert_type(bits, F32)


def _dsa_body(q_ref, iq_ref, smq_ref, k_ref, v_ref, smk_ref, ikg_ref, o_ref,
              ikn_ref, vt_ref, s_ref, bias_ref, jsel_ref, m_ref, l_ref, acc_ref, *, seq, k_top):
    qb = pl.program_id(1)
    kc = DSA_KC
    nq = DSA_QB
    rep = DSA_HEADS // DSA_KV_HEADS

    @pl.when(qb == 0)
    def _():
        def prep(i, carry):
            rs = pl.ds(pl.multiple_of(i * kc, kc), kc)
            sm = smk_ref[rs, :]
            lane = lax.broadcasted_iota(I32, sm.shape, 1)
            isk = lane < IDX_DIM
            mean = jnp.sum(jnp.where(isk, sm, 0.0), axis=-1, keepdims=True) * (1.0 / IDX_DIM)
            xc = jnp.where(isk, sm - mean, 0.0)
            var = jnp.sum(xc * xc, axis=-1, keepdims=True) * (1.0 / IDX_DIM)
            y = xc * lax.rsqrt(var + EPS) * ikg_ref[...]
            ikn_ref[0, rs, :] = y.astype(BF16)
            ikn_ref[1, rs, :] = pltpu.roll(y, IDX_DIM, axis=1).astype(BF16)
            vv = v_ref[rs, :].astype(F32)
            for g in range(DSA_KV_HEADS):
                vt_ref[g, i] = vv[:, g * DSA_DH:(g + 1) * DSA_DH].T.astype(BF16)
            return carry
        lax.fori_loop(0, seq // kc, prep, 0)

    q_lo = qb * nq
    n_kc = (q_lo + nq + kc - 1) // kc
    qpos = q_lo + lax.broadcasted_iota(I32, (1, nq), 1)

    iq = iq_ref[...]
    n_pair = IDX_HEADS // 2
    iq_rows = jnp.concatenate([iq[:, p * LANES:(p + 1) * LANES] for p in range(n_pair)], axis=0)
    iw_t = smq_ref[...].T * ((IDX_HEADS ** -0.5) * (IDX_DIM ** -0.5))

    def score_chunk(i, carry):
        rs = pl.ds(pl.multiple_of(i * kc, kc), kc)
        lo = _dot_nt(ikn_ref[0, rs, :], iq_rows)
        hi = _dot_nt(ikn_ref[1, rs, :], iq_rows)
        sc = jnp.zeros((kc, nq), F32)
        for p in range(n_pair):
            cs = slice(p * nq, (p + 1) * nq)
            sc = sc + iw_t[SM_IW + 2 * p:SM_IW + 2 * p + 1, :] * jnp.maximum(lo[:, cs], 0.0)
            sc = sc + iw_t[SM_IW + 2 * p + 1:SM_IW + 2 * p + 2, :] * jnp.maximum(hi[:, cs], 0.0)
        kpos = i * kc + lax.broadcasted_iota(I32, (kc, nq), 0)
        s_ref[rs, :] = jnp.where(kpos <= qpos, sc + 0.0, -jnp.inf)
        return carry
    lax.fori_loop(0, n_kc, score_chunk, 0)

    def count_ge(thr):
        def body(i, cnt):
            rs = pl.ds(pl.multiple_of(i * kc, kc), kc)
            hit = jnp.where(s_ref[rs, :] >= thr, 1.0, 0.0)
            return cnt + jnp.sum(hit.reshape(kc // SUBLANES, SUBLANES, nq), axis=0)
        cnt = lax.fori_loop(0, n_kc, body, jnp.zeros((SUBLANES, nq), F32))
        return jnp.sum(cnt, axis=0, keepdims=True)

    kf = jnp.float32(k_top)

    def bisect(_, carry):
        lo, hi = carry
        mid = (lo >> 1) + (hi >> 1) + (lo & hi & 1)
        ok = count_ge(_key_to_f32(mid)) >= kf
        return jnp.where(ok, mid, lo), jnp.where(ok, hi, mid)
    lo0 = jnp.full((1, nq), _KEY_NEG_INF, I32)
    hi0 = jnp.full((1, nq), _KEY_POS_INF_P1, I32)
    lo, _ = lax.fori_loop(0, 32, bisect, (lo0, hi0))
    thr = _key_to_f32(lo)

    def count_gt_ge(i, carry):
        c_gt, c_ge = carry
        rs = pl.ds(pl.multiple_of(i * kc, kc), kc)
        s = s_ref[rs, :]
        gt = jnp.where(s > thr, 1.0, 0.0).reshape(kc // SUBLANES, SUBLANES, nq)
        ge = jnp.where(s >= thr, 1.0, 0.0).reshape(kc // SUBLANES, SUBLANES, nq)
        return c_gt + jnp.sum(gt, axis=0), c_ge + jnp.sum(ge, axis=0)
    z8 = jnp.zeros((SUBLANES, nq), F32)
    c_gt, c_ge = lax.fori_loop(0, n_kc, count_gt_ge, (z8, z8))
    n_gt = jnp.sum(c_gt, axis=0, keepdims=True)
    n_ge = jnp.sum(c_ge, axis=0, keepdims=True)
    need = kf - n_gt
    jsel_ref[...] = jnp.full((1, nq), seq, I32)

    @pl.when(jnp.max(n_ge) > kf)
    def _():
        def count_eq_upto(jmax):
            def body(i, cnt):
                rs = pl.ds(pl.multiple_of(i * kc, kc), kc)
                kpos = i * kc + lax.broadcasted_iota(I32, (kc, nq), 0)
                hit = jnp.where(s_ref[rs, :] == thr, jnp.where(kpos <= jmax, 1.0, 0.0), 0.0)
                return cnt + jnp.sum(hit.reshape(kc // SUBLANES, SUBLANES, nq), axis=0)
            cnt = lax.fori_loop(0, n_kc, body, jnp.zeros((SUBLANES, nq), F32))
            return jnp.sum(cnt, axis=0, keepdims=True)

        def jbisect(_, carry):
            lo_j, hi_j = carry
            mid = (lo_j + hi_j) >> 1
            ok = count_eq_upto(mid) >= need
            return jnp.where(ok, lo_j, mid), jnp.where(ok, mid, hi_j)
        n_iter = int(np.ceil(np.log2(seq))) + 1
        lo_j0 = jnp.full((1, nq), -1, I32)
        hi_j0 = jnp.full((1, nq), seq - 1, I32)
        _, hi_j = lax.fori_loop(0, n_iter, jbisect, (lo_j0, hi_j0))
        jsel_ref[...] = hi_j

    jsel = jsel_ref[...]

    def bias_chunk(i, carry):
        rs = pl.ds(pl.multiple_of(i * kc, kc), kc)
        s = s_ref[rs, :]
        kpos = i * kc + lax.broadcasted_iota(I32, (kc, nq), 0)
        tie = jnp.where(s == thr, jnp.where(kpos <= jsel, 0.0, NEG_BIG), NEG_BIG)
        sel = jnp.where(s > thr, 0.0, tie)
        bias_ref[rs, :] = jnp.where(kpos <= qpos, sel, NEG_BIG)
        return carry
    lax.fori_loop(0, n_kc, bias_chunk, 0)

    scale = DSA_DH ** -0.5
    qall = q_ref[...]
    q_rows = [jnp.concatenate([qall[:, (g * rep + r) * DSA_DH:(g * rep + r + 1) * DSA_DH]
                               for r in range(rep)], axis=0) for g in range(DSA_KV_HEADS)]
    m_ref[...] = jnp.full(m_ref.shape, NEG_BIG, F32)
    l_ref[...] = jnp.zeros(l_ref.shape, F32)
    acc_ref[...] = jnp.zeros(acc_ref.shape, F32)

    def attn_chunk(i, carry):
        rs = pl.ds(pl.multiple_of(i * kc, kc), kc)
        bias = bias_ref[rs, :]
        bias_w = jnp.concatenate([bias] * rep, axis=1)
        for g in range(DSA_KV_HEADS):
            kg = k_ref[rs, g * DSA_DH:(g + 1) * DSA_DH]
            logit = _dot_nt(kg, q_rows[g]) * scale + bias_w
            m_old = m_ref[g]
            m_new = jnp.maximum(m_old, jnp.max(logit, axis=0, keepdims=True))
            alpha = jnp.exp(m_old - m_new)
            p = jnp.exp(logit - m_new)
            l_ref[g] = l_ref[g] * alpha + jnp.sum(p, axis=0, keepdims=True)
            acc_ref[g] = acc_ref[g] * alpha + _dot(vt_ref[g, i], p.astype(BF16))
            m_ref[g] = m_new
        return carry
    lax.fori_loop(0, n_kc, attn_chunk, 0)

    for g in range(DSA_KV_HEADS):
        ot = acc_ref[g] / l_ref[g]
        for r in range(rep):
            hh = g * rep + r
            o_ref[:, hh * DSA_DH:(hh + 1) * DSA_DH] = ot[:, r * nq:(r + 1) * nq].T.astype(o_ref.dtype)


def _dsa(proj, small, ikg):
    b, s, _ = proj.shape
    nq = DSA_QB
    k_top = min(TOPK_MAX, s // 4)
    rep = DSA_HEADS // DSA_KV_HEADS
    return pl.pallas_call(
        functools.partial(_dsa_body, seq=s, k_top=k_top),
        grid=(b, s // nq),
        in_specs=[
            pl.BlockSpec((None, nq, DSA_HEADS * DSA_DH), lambda bi, qb: (bi, qb, COL_DQ // (DSA_HEADS * DSA_DH))),
            pl.BlockSpec((None, nq, IDX_HEADS * IDX_DIM), lambda bi, qb: (bi, qb, COL_IQ // (IDX_HEADS * IDX_DIM))),
            pl.BlockSpec((None, nq, LANES), lambda bi, qb: (bi, qb, 0)),
            pl.BlockSpec((None, s, DSA_KV_HEADS * DSA_DH), lambda bi, qb: (bi, 0, COL_DK // (DSA_KV_HEADS * DSA_DH))),
            pl.BlockSpec((None, s, DSA_KV_HEADS * DSA_DH), lambda bi, qb: (bi, 0, COL_DV // (DSA_KV_HEADS * DSA_DH))),
            pl.BlockSpec((None, s, LANES), lambda bi, qb: (bi, 0, 0)),
            pl.BlockSpec((1, LANES), lambda bi, qb: (0, 0)),
        ],
        out_specs=pl.BlockSpec((None, nq, DSA_HEADS * DSA_DH), lambda bi, qb: (bi, qb, 0)),
        out_shape=jax.ShapeDtypeStruct((b, s, DSA_HEADS * DSA_DH), BF16),
        scratch_shapes=[
            pltpu.VMEM((2, s, LANES), BF16),
            pltpu.VMEM((DSA_KV_HEADS, s // DSA_KC, DSA_DH, DSA_KC), BF16),
            pltpu.VMEM((s, nq), F32),
            pltpu.VMEM((s, nq), F32),
            pltpu.VMEM((1, nq), I32),
            pltpu.VMEM((DSA_KV_HEADS, 1, rep * nq), F32),
            pltpu.VMEM((DSA_KV_HEADS, 1, rep * nq), F32),
            pltpu.VMEM((DSA_KV_HEADS, DSA_DH, rep * nq), F32),
        ],
        compiler_params=_cparams(("arbitrary", "arbitrary")),
        name="dsa",
    )(proj, proj, small, proj, proj, small, ikg)


def _merge_body(x_ref, ga_ref, gb_ref, ba_ref, bb_ref, og_ref, od_ref, wo_ref, h_ref):
    sa = _sigmoid(ga_ref[...].astype(F32) + ba_ref[...])
    sb = _sigmoid(gb_ref[...].astype(F32) + bb_ref[...])
    mixed = sa * og_ref[...].astype(F32) + sb * od_ref[...].astype(F32)
    h_ref[...] = x_ref[...] + _dot(mixed.astype(BF16), wo_ref[...])


def _merge(x2, proj2, bias_a, bias_b, o_gdn2, o_dsa2, w_out, *, tm=512):
    n = x2.shape[0]
    const = lambda i: (0, 0)
    row = lambda i: (i, 0)
    return pl.pallas_call(
        _merge_body,
        grid=(n // tm,),
        in_specs=[
            pl.BlockSpec((tm, D_MODEL), row),
            pl.BlockSpec((tm, D_MODEL), lambda i: (i, COL_GA // D_MODEL)),
            pl.BlockSpec((tm, D_MODEL), lambda i: (i, COL_GB // D_MODEL)),
            pl.BlockSpec((1, D_MODEL), const),
            pl.BlockSpec((1, D_MODEL), const),
            pl.BlockSpec((tm, D_MODEL), row),
            pl.BlockSpec((tm, D_MODEL), row),
            pl.BlockSpec((D_MODEL, D_MODEL), const),
        ],
        out_specs=pl.BlockSpec((tm, D_MODEL), row),
        out_shape=jax.ShapeDtypeStruct((n, D_MODEL), F32),
        compiler_params=_cparams(("arbitrary",)),
        name="merge",
    )(x2, proj2, proj2, bias_a, bias_b, o_gdn2, o_dsa2, w_out)


FFN_CHUNK = 256


def _ffn_body(h_ref, g2_ref, wup_ref, cw_ref, cb_ref, wdn_ref, gf_ref, o_ref,
              carry_ref, xx_ref, *, tm):
    t = pl.program_id(1)

    @pl.when(t == 0)
    def _():
        carry_ref[...] = jnp.zeros_like(carry_ref)

    h = h_ref[...]
    hn = ((h * lax.rsqrt(jnp.mean(h * h, axis=-1, keepdims=True) + EPS)) * g2_ref[...]).astype(BF16)

    def conv_cols(c0):
        cs = slice(c0, c0 + FFN_CHUNK)
        up = _dot(hn, wup_ref[:, cs])
        xx_ref[0:SUBLANES, :] = carry_ref[:, cs]
        xx_ref[SUBLANES:SUBLANES + tm, :] = up
        carry_ref[:, cs] = up[tm - SUBLANES:tm, :]
        w = cw_ref[:, cs]
        y = w[FFN_CONV - 1:FFN_CONV, :] * up + cb_ref[:, cs]
        for j in range(FFN_CONV - 1):
            s0 = SUBLANES - (FFN_CONV - 1) + j
            y = y + w[j:j + 1, :] * xx_ref[s0:s0 + tm, :]
        return y

    acc = jnp.zeros((tm, D_MODEL), F32)
    for c in range(D_FF // FFN_CHUNK):
        gate = conv_cols(c * FFN_CHUNK)
        val = conv_cols(D_FF + c * FFN_CHUNK)
        act = (_silu(gate) * val).astype(BF16)
        acc = acc + _dot(act, wdn_ref[c * FFN_CHUNK:(c + 1) * FFN_CHUNK, :])
    h2 = h + acc
    o_ref[...] = (h2 * lax.rsqrt(jnp.mean(h2 * h2, axis=-1, keepdims=True) + EPS)) * gf_ref[...]


def _ffn(h1, g2, w_up, conv_w, conv_b, w_down, gf, *, tm=256):
    b, s, _ = h1.shape
    const = lambda bi, t: (0, 0)
    return pl.pallas_call(
        functools.partial(_ffn_body, tm=tm),
        grid=(b, s // tm),
        in_specs=[
            pl.BlockSpec((None, tm, D_MODEL), lambda bi, t: (bi, t, 0)),
            pl.BlockSpec((1, D_MODEL), const),
            pl.BlockSpec((D_MODEL, 2 * D_FF), const, pipeline_mode=pl.Buffered(1)),
            pl.BlockSpec((FFN_CONV, 2 * D_FF), const),
            pl.BlockSpec((1, 2 * D_FF), const),
            pl.BlockSpec((D_FF, D_MODEL), const, pipeline_mode=pl.Buffered(1)),
            pl.BlockSpec((1, D_MODEL), const),
        ],
        out_specs=pl.BlockSpec((None, tm, D_MODEL), lambda bi, t: (bi, t, 0)),
        out_shape=jax.ShapeDtypeStruct((b, s, D_MODEL), F32),
        scratch_shapes=[
            pltpu.VMEM((SUBLANES, 2 * D_FF), F32),
            pltpu.VMEM((SUBLANES + tm, FFN_CHUNK), F32),
        ],
        compiler_params=_cparams(("arbitrary", "arbitrary")),
        name="ffn",
    )(h1, g2, w_up, conv_w, conv_b, w_down, gf)


def _rearranged_w_in(w):
    o = np.cumsum([0, 3072, 1024, 8, 8, 1024, 256, 256, 512, 64, 8, 1024, 1024])
    g_qkv, g_z, g_a, g_b, d_q, d_k, d_v, i_q, i_k, i_w, gate_a, gate_b = (
        w[:, int(o[i]):int(o[i + 1])] for i in range(12))
    big = jnp.concatenate([g_qkv, g_z, d_q, d_k, d_v, i_q, gate_a, gate_b], axis=1).astype(BF16)
    pad = jnp.zeros((w.shape[0], LANES - (IDX_DIM + 3 * 8)), w.dtype)
    small = jnp.concatenate([i_k, g_a, g_b, i_w, pad], axis=1).astype(BF16)
    return big, small


def _lane_vec(v, offset):
    return jnp.zeros((1, LANES), F32).at[0, offset:offset + v.shape[0]].set(v.astype(F32))


def kernel(x, norm1_g, w_in, gdn_conv_w, gdn_a_log, gdn_dt_bias, gdn_norm_g, idx_k_norm_g,
           branch_gate_b, w_out, norm2_g, ffn_w_up, ffn_conv_w, ffn_conv_b, ffn_w_down, final_g):
    b, s, d = x.shape
    depth = norm1_g.shape[0]
    assert depth == 1, "the final RMSNorm is fused into the (single) channel-mixer call"
    h = x
    for l in range(depth):
        w_big, w_small = _rearranged_w_in(w_in[l])
        proj2, small2 = _inproj(h.reshape(b * s, d), norm1_g[l][None, :], w_big, w_small)
        proj = proj2.reshape(b, s, N_BIG)
        small = small2.reshape(b, s, LANES)
        o_gdn = _gdn(proj, small, gdn_conv_w[l], _lane_vec(gdn_a_log[l], SM_A),
                     _lane_vec(gdn_dt_bias[l], SM_A), gdn_norm_g[l][None, :].astype(F32))
        o_dsa = _dsa(proj, small, _lane_vec(idx_k_norm_g[l], SM_IK))
        gb = branch_gate_b[l]
        h1 = _merge(h.reshape(b * s, d), proj2, gb[None, :D_MODEL], gb[None, D_MODEL:],
                    o_gdn.reshape(b * s, -1), o_dsa.reshape(b * s, -1), w_out[l].astype(BF16))
        h = _ffn(h1.reshape(b, s, d), norm2_g[l][None, :], ffn_w_up[l].astype(BF16), ffn_conv_w[l],
                 ffn_conv_b[l][None, :], ffn_w_down[l].astype(BF16), final_g[None, :])
    return h
```

```python
import functools

import jax
import jax.numpy as jnp
import numpy as np
from jax import lax
from jax.experimental import pallas as pl
from jax.experimental.pallas import tpu as pltpu

F32 = jnp.float32
BF16 = jnp.bfloat16
I32 = jnp.int32

LANES = 128
SUBLANES = 8
VMEM_LIMIT = 56 * 1024 * 1024

D_MODEL = 1024
GDN_HEADS = 8
GDN_D = 128
GDN_CONV = 4
DSA_HEADS = 8
DSA_KV_HEADS = 2
DSA_DH = 128
IDX_HEADS = 8
IDX_DIM = 64
TOPK_MAX = 256
D_FF = 2816
FFN_CONV = 3
EPS = 1e-6

COL_QKV = 0
COL_Z = 3072
COL_DQ = 4096
COL_DK = 5120
COL_DV = 5376
COL_IQ = 5632
COL_GA = 6144
COL_GB = 7168
N_BIG = 8192
SM_IK = 0
SM_A = 64
SM_B = 72
SM_IW = 80

NEG_BIG = -1e30


def _cparams(sem):
    return pltpu.CompilerParams(dimension_semantics=sem, vmem_limit_bytes=VMEM_LIMIT)


def _dot(a, b):
    return jnp.dot(a, b, preferred_element_type=F32)


def _dot_nt(a, b):
    return lax.dot_general(a, b, (((1,), (1,)), ((), ())), preferred_element_type=F32)


def _dot_tn(a, b):
    return lax.dot_general(a, b, (((0,), (0,)), ((), ())), preferred_element_type=F32)


def _split3(x):
    h = x.astype(BF16)
    r = x - h.astype(F32)
    m = r.astype(BF16)
    l = (r - m.astype(F32)).astype(BF16)
    return h, m, l


def _sigmoid(x):
    return 1.0 / (1.0 + jnp.exp(-x))


def _silu(x):
    return x * _sigmoid(x)


def _softplus(x):
    return jnp.maximum(x, 0.0) + jnp.log(1.0 + jnp.exp(-jnp.abs(x)))


def _inproj_body(x_ref, g_ref, w_ref, ws_ref, proj_ref, small_ref, *, col_chunk):
    x = x_ref[...]
    ms = jnp.mean(x * x, axis=-1, keepdims=True)
    u = ((x * lax.rsqrt(ms + EPS)) * g_ref[...]).astype(BF16)
    for c in range(N_BIG // col_chunk):
        sl = slice(c * col_chunk, (c + 1) * col_chunk)
        proj_ref[:, sl] = _dot(u, w_ref[:, sl]).astype(BF16)
    small_ref[...] = _dot(u, ws_ref[...])


def _inproj(x2, g, w_big, w_small, *, tm=256):
    n = x2.shape[0]
    const = lambda i: (0, 0)
    return pl.pallas_call(
        functools.partial(_inproj_body, col_chunk=1024),
        grid=(n // tm,),
        in_specs=[
            pl.BlockSpec((tm, D_MODEL), lambda i: (i, 0)),
            pl.BlockSpec((1, D_MODEL), const),
            pl.BlockSpec((D_MODEL, N_BIG), const, pipeline_mode=pl.Buffered(1)),
            pl.BlockSpec((D_MODEL, LANES), const, pipeline_mode=pl.Buffered(1)),
        ],
        out_specs=[
            pl.BlockSpec((tm, N_BIG), lambda i: (i, 0)),
            pl.BlockSpec((tm, LANES), lambda i: (i, 0)),
        ],
        out_shape=[
            jax.ShapeDtypeStruct((n, N_BIG), BF16),
            jax.ShapeDtypeStruct((n, LANES), F32),
        ],
        compiler_params=_cparams(("arbitrary",)),
        name="inproj",
    )(x2, g, w_big, w_small)


GDN_C = 128


def _tri_inverse(a_list):
    n = a_list[0].shape[0]
    row = lax.broadcasted_iota(I32, (n, n), 0)
    col = lax.broadcasted_iota(I32, (n, n), 1)
    eye = jnp.where(row == col, 1.0, 0.0).astype(F32)

    def same_block(size):
        sh = size.bit_length() - 1
        return (row >> sh) == (col >> sh)

    def square(ps):
        return [_dot(p, p).astype(BF16) for p in ps]

    def grow(ts, ps):
        return [t + _dot(t.astype(BF16), p) for t, p in zip(ts, ps)]

    p1 = [jnp.where(same_block(16), a, 0.0) for a in a_list]
    p2s = square([p.astype(BF16) for p in p1])
    ts = [eye - p for p in p1]
    p4s = square(p2s)
    ts = grow(ts, p2s)
    p8s = square(p4s)
    ts = grow(ts, p4s)
    ts = grow(ts, p8s)
    size = 16
    while size < n:
        in_pair = same_block(2 * size) & jnp.logical_not(same_block(size))
        offs = [jnp.where(in_pair, a, 0.0).astype(BF16) for a in a_list]
        tss = [t.astype(BF16) for t in ts]
        mids = [_dot(t2, o).astype(BF16) for t2, o in zip(tss, offs)]
        ts = [t - _dot(m, t2) for t, m, t2 in zip(ts, mids, tss)]
        size *= 2
    return ts


def _gdn_body(qp_ref, kp_ref, vp_ref, z_ref, sm_ref, wq_ref, wk_ref, wv_ref,
              alog_ref, dtb_ref, ng_ref, o_ref,
              state_ref, halo_ref, xx_ref, *, tb, hg):
    hgi = pl.program_id(1)
    t = pl.program_id(2)

    @pl.when(t == 0)
    def _():
        state_ref[...] = jnp.zeros_like(state_ref)
        halo_ref[...] = jnp.zeros_like(halo_ref)

    def conv_silu(idx, x_ref, w_ref):
        x = x_ref[...].astype(F32)
        xx_ref[0:SUBLANES, :] = halo_ref[idx]
        xx_ref[SUBLANES:SUBLANES + tb, :] = x
        halo_ref[idx] = x[tb - SUBLANES:tb, :]
        w = w_ref[...]
        y = w[GDN_CONV - 1:GDN_CONV, :] * x
        for j in range(GDN_CONV - 1):
            s0 = SUBLANES - (GDN_CONV - 1) + j
            y = y + w[j:j + 1, :] * xx_ref[s0:s0 + tb, :]
        return _silu(y)

    q_all = conv_silu(0, qp_ref, wq_ref)
    k_all = conv_silu(1, kp_ref, wk_ref)
    v_all = conv_silu(2, vp_ref, wv_ref)

    sm = sm_ref[...]
    ld_all = -jnp.exp(alog_ref[...]) * _softplus(sm + dtb_ref[...])
    beta_all = _sigmoid(sm)

    c = GDN_C
    row = lax.broadcasted_iota(I32, (c, c), 0)
    col = lax.broadcasted_iota(I32, (c, c), 1)
    lane = lax.broadcasted_iota(I32, (c, LANES), 1)
    tri_incl = jnp.where(row >= col, 1.0, 0.0).astype(BF16)
    ng = ng_ref[...]

    for ci in range(tb // c):
        rs = slice(ci * c, (ci + 1) * c)
        l1, l2, l3 = _split3(ld_all[rs])
        g_all = _dot(tri_incl, l1) + (_dot(tri_incl, l2) + _dot(tri_incl, l3))
        heads = range(hg)
        cols = [slice(j * GDN_D, (j + 1) * GDN_D) for j in heads]
        gc, bc, gamma, eg, g_last, qc, kc, kb, kc16 = ([None] * hg for _ in range(9))
        for j in heads:
            h = hgi * hg + j
            gcol = jnp.sum(jnp.where(lane == SM_A + h, g_all, 0.0), axis=-1, keepdims=True)
            bc[j] = jnp.sum(jnp.where(lane == SM_B + h, beta_all[rs], 0.0), axis=-1, keepdims=True)
            gc[j] = jnp.broadcast_to(gcol, (c, c))
            gamma[j] = jnp.exp(jnp.where(row >= col, gc[j] - gc[j].T, -jnp.inf))
            eg[j] = jnp.exp(gc[j])
            g_last[j] = gc[j][c - 1:c, :]
            q_h = q_all[rs, cols[j]]
            k_h = k_all[rs, cols[j]]
            qc[j] = q_h * (lax.rsqrt(jnp.sum(q_h * q_h, axis=-1, keepdims=True) + EPS) * (GDN_D ** -0.5))
            kc[j] = k_h * lax.rsqrt(jnp.sum(k_h * k_h, axis=-1, keepdims=True) + EPS)
            kb[j] = kc[j] * bc[j]
            kc16[j] = kc[j].astype(BF16)
        a = [jnp.where(row > col, _dot_nt(kb[j].astype(BF16), kc16[j]) * gamma[j], 0.0) for j in heads]
        att = [(_dot_nt(qc[j].astype(BF16), kc16[j]) * gamma[j]).astype(BF16) for j in heads]
        tinv = [t.astype(BF16) for t in _tri_inverse(a)]
        u = [_dot(tinv[j], (v_all[rs, cols[j]] * bc[j]).astype(BF16)) for j in heads]
        w = [_dot(tinv[j], (kb[j] * eg[j]).astype(BF16)).astype(BF16) for j in heads]
        qg = [(qc[j] * eg[j]).astype(BF16) for j in heads]
        kd = [(kc[j] * jnp.exp(g_last[j] - gc[j])).astype(BF16) for j in heads]

        s_old = [state_ref[j] for j in heads]
        s16 = [s.astype(BF16) for s in s_old]
        v_new16 = [(u[j] - _dot(w[j], s16[j])).astype(BF16) for j in heads]
        o = [_dot(qg[j], s16[j]) + _dot(att[j], v_new16[j]) for j in heads]
        for j in heads:
            state_ref[j] = s_old[j] * jnp.exp(g_last[j]) + _dot_tn(kd[j], v_new16[j])
        for j in heads:
            y = o[j] * lax.rsqrt(jnp.mean(o[j] * o[j], axis=-1, keepdims=True) + EPS) * ng
            o_ref[rs, cols[j]] = (y * _silu(z_ref[rs, cols[j]].astype(F32))).astype(o_ref.dtype)


def _gdn(proj, small, conv_w, alog_v, dtb_v, ng, *, tb=128, hg=8):
    b, s, _ = proj.shape
    nh = GDN_HEADS
    w = hg * GDN_D
    ng_blocks = nh // hg
    hq = COL_QKV // w
    hz = COL_Z // w
    return pl.pallas_call(
        functools.partial(_gdn_body, tb=tb, hg=hg),
        grid=(b, ng_blocks, s // tb),
        in_specs=[
            pl.BlockSpec((None, tb, w), lambda bi, g, t: (bi, t, hq + g)),
            pl.BlockSpec((None, tb, w), lambda bi, g, t: (bi, t, hq + ng_blocks + g)),
            pl.BlockSpec((None, tb, w), lambda bi, g, t: (bi, t, hq + 2 * ng_blocks + g)),
            pl.BlockSpec((None, tb, w), lambda bi, g, t: (bi, t, hz + g)),
            pl.BlockSpec((None, tb, LANES), lambda bi, g, t: (bi, t, 0)),
            pl.BlockSpec((GDN_CONV, w), lambda bi, g, t: (0, g)),
            pl.BlockSpec((GDN_CONV, w), lambda bi, g, t: (0, ng_blocks + g)),
            pl.BlockSpec((GDN_CONV, w), lambda bi, g, t: (0, 2 * ng_blocks + g)),
            pl.BlockSpec((1, LANES), lambda bi, g, t: (0, 0)),
            pl.BlockSpec((1, LANES), lambda bi, g, t: (0, 0)),
            pl.BlockSpec((1, LANES), lambda bi, g, t: (0, 0)),
        ],
        out_specs=pl.BlockSpec((None, tb, w), lambda bi, g, t: (bi, t, g)),
        out_shape=jax.ShapeDtypeStruct((b, s, nh * GDN_D), BF16),
        scratch_shapes=[
            pltpu.VMEM((hg, GDN_D, GDN_D), F32),
            pltpu.VMEM((3, SUBLANES, w), F32),
            pltpu.VMEM((SUBLANES + tb, w), F32),
        ],
        compiler_params=_cparams(("arbitrary", "arbitrary", "arbitrary")),
        name="gdn",
    )(proj, proj, proj, proj, small, conv_w, conv_w, conv_w, alog_v, dtb_v, ng)


DSA_QB = 128
DSA_KC = 512
DSA_ONES = 16

_KEY_NEG_INF = np.int32(np.array(0xFF800000, np.uint32).view(np.int32) ^ np.int32(0x7FFFFFFF))
_KEY_POS_INF_P1 = np.int32(0x7F800001)


def _key_to_f32(key):
    bits = key ^ ((key >> 31) & jnp.int32(0x7FFFFFFF))
    return lax.bitcast_convert_type(bits, F32)


def _tree_sum(xs):
    xs = list(xs)
    while len(xs) > 1:
        nxt = [xs[i] + xs[i + 1] for i in range(0, len(xs) - 1, 2)]
        if len(xs) % 2:
            nxt.append(xs[-1])
        xs = nxt
    return xs[0]


def _count_rows(hit):
    rows, n = hit.shape
    parts = hit.reshape(rows // SUBLANES, SUBLANES, n)
    return _tree_sum([parts[i] for i in range(rows // SUBLANES)])


def _dsa_body(q_ref, iq_ref, smq_ref, k_ref, v_ref, smk_ref, ikg_ref, o_ref,
              ikn_ref, vt_ref, s_ref, bias_ref, jsel_ref, m_ref, acc_ref, *, seq, k_top):
    qb = pl.program_id(1)
    kc = DSA_KC
    nq = DSA_QB
    rep = DSA_HEADS // DSA_KV_HEADS

    @pl.when(qb == 0)
    def _():
        def prep(i, carry):
            rs = pl.ds(pl.multiple_of(i * kc, kc), kc)
            sm = smk_ref[rs, :]
            lane = lax.broadcasted_iota(I32, sm.shape, 1)
            isk = lane < IDX_DIM
            mean = jnp.sum(jnp.where(isk, sm, 0.0), axis=-1, keepdims=True) * (1.0 / IDX_DIM)
            xc = jnp.where(isk, sm - mean, 0.0)
            var = jnp.sum(xc * xc, axis=-1, keepdims=True) * (1.0 / IDX_DIM)
            y = xc * lax.rsqrt(var + EPS) * ikg_ref[...]
            ikn_ref[0, rs, :] = y.astype(BF16)
            ikn_ref[1, rs, :] = pltpu.roll(y, IDX_DIM, axis=1).astype(BF16)
            vv = v_ref[rs, :].astype(F32)
            for g in range(DSA_KV_HEADS):
                vt_ref[g, i, 0:DSA_DH, :] = vv[:, g * DSA_DH:(g + 1) * DSA_DH].T.astype(BF16)
                vt_ref[g, i, DSA_DH:DSA_DH + DSA_ONES, :] = jnp.ones((DSA_ONES, kc), BF16)
            return carry
        lax.fori_loop(0, seq // kc, prep, 0)

    q_lo = qb * nq
    n_kc = (q_lo + nq + kc - 1) // kc
    qpos = q_lo + lax.broadcasted_iota(I32, (1, nq), 1)

    iq = iq_ref[...]
    n_pair = IDX_HEADS // 2
    iq_rows = jnp.concatenate([iq[:, p * LANES:(p + 1) * LANES] for p in range(n_pair)], axis=0)
    iw_t = smq_ref[...].T * ((IDX_HEADS ** -0.5) * (IDX_DIM ** -0.5))

    def score_chunk(i, carry):
        rs = pl.ds(pl.multiple_of(i * kc, kc), kc)
        lo = _dot_nt(ikn_ref[0, rs, :], iq_rows)
        hi = _dot_nt(ikn_ref[1, rs, :], iq_rows)
        sc = jnp.zeros((kc, nq), F32)
        for p in range(n_pair):
            cs = slice(p * nq, (p + 1) * nq)
            sc = sc + iw_t[SM_IW + 2 * p:SM_IW + 2 * p + 1, :] * jnp.maximum(lo[:, cs], 0.0)
            sc = sc + iw_t[SM_IW + 2 * p + 1:SM_IW + 2 * p + 2, :] * jnp.maximum(hi[:, cs], 0.0)
        kpos = i * kc + lax.broadcasted_iota(I32, (kc, nq), 0)
        s_ref[rs, :] = jnp.where(kpos <= qpos, sc + 0.0, -jnp.inf)
        return carry
    lax.fori_loop(0, n_kc, score_chunk, 0)

    def count_ge(thr):
        def body(i, cnt):
            rs = pl.ds(pl.multiple_of(i * kc, kc), kc)
            return cnt + _count_rows(jnp.where(s_ref[rs, :] >= thr, 1.0, 0.0))
        cnt = lax.fori_loop(0, n_kc, body, jnp.zeros((SUBLANES, nq), F32))
        return jnp.sum(cnt, axis=0, keepdims=True)

    kf = jnp.float32(k_top)

    def bisect(_, carry):
        lo, hi = carry
        mid = (lo >> 1) + (hi >> 1) + (lo & hi & 1)
        ok = count_ge(_key_to_f32(mid)) >= kf
        return jnp.where(ok, mid, lo), jnp.where(ok, hi, mid)
    lo0 = jnp.full((1, nq), _KEY_NEG_INF, I32)
    hi0 = jnp.full((1, nq), _KEY_POS_INF_P1, I32)
    lo, _ = lax.fori_loop(0, 32, bisect, (lo0, hi0))
    thr = _key_to_f32(lo)

    def count_gt_ge(i, carry):
        c_gt, c_ge = carry
        rs = pl.ds(pl.multiple_of(i * kc, kc), kc)
        s = s_ref[rs, :]
        return (c_gt + _count_rows(jnp.where(s > thr, 1.0, 0.0)),
                c_ge + _count_rows(jnp.where(s >= thr, 1.0, 0.0)))
    z8 = jnp.zeros((SUBLANES, nq), F32)
    c_gt, c_ge = lax.fori_loop(0, n_kc, count_gt_ge, (z8, z8))
    n_gt = jnp.sum(c_gt, axis=0, keepdims=True)
    n_ge = jnp.sum(c_ge, axis=0, keepdims=True)
    need = kf - n_gt
    jsel_ref[...] = jnp.full((1, nq), seq, I32)

    @pl.when(jnp.max(n_ge) > kf)
    def _():
        def count_eq_upto(jmax):
            def body(i, cnt):
                rs = pl.ds(pl.multiple_of(i * kc, kc), kc)
                kpos = i * kc + lax.broadcasted_iota(I32, (kc, nq), 0)
                hit = jnp.where(s_ref[rs, :] == thr, jnp.where(kpos <= jmax, 1.0, 0.0), 0.0)
                return cnt + _count_rows(hit)
            cnt = lax.fori_loop(0, n_kc, body, jnp.zeros((SUBLANES, nq), F32))
            return jnp.sum(cnt, axis=0, keepdims=True)

        def jbisect(_, carry):
            lo_j, hi_j = carry
            mid = (lo_j + hi_j) >> 1
            ok = count_eq_upto(mid) >= need
            return jnp.where(ok, lo_j, mid), jnp.where(ok, mid, hi_j)
        n_iter = int(np.ceil(np.log2(seq))) + 1
        lo_j0 = jnp.full((1, nq), -1, I32)
        hi_j0 = jnp.full((1, nq), seq - 1, I32)
        _, hi_j = lax.fori_loop(0, n_iter, jbisect, (lo_j0, hi_j0))
        jsel_ref[...] = hi_j

    jsel = jsel_ref[...]

    def bias_chunk(i, carry):
        rs = pl.ds(pl.multiple_of(i * kc, kc), kc)
        s = s_ref[rs, :]
        kpos = i * kc + lax.broadcasted_iota(I32, (kc, nq), 0)
        tie = jnp.where(s == thr, jnp.where(kpos <= jsel, 0.0, NEG_BIG), NEG_BIG)
        sel = jnp.where(s > thr, 0.0, tie)
        bias_ref[rs, :] = jnp.where(kpos <= qpos, sel, NEG_BIG)
        return carry
    lax.fori_loop(0, n_kc, bias_chunk, 0)

    qall = (q_ref[...].astype(F32) * (DSA_DH ** -0.5 * np.log2(np.e))).astype(BF16)
    q_rows = [jnp.concatenate([qall[:, (g * rep + r) * DSA_DH:(g * rep + r + 1) * DSA_DH]
                               for r in range(rep)], axis=0) for g in range(DSA_KV_HEADS)]
    m_ref[...] = jnp.full(m_ref.shape, NEG_BIG, F32)
    acc_ref[...] = jnp.zeros(acc_ref.shape, F32)

    def attn_chunk(i, carry):
        rs = pl.ds(pl.multiple_of(i * kc, kc), kc)
        bias = bias_ref[rs, :]
        bias_w = jnp.concatenate([bias] * rep, axis=1)
        for g in range(DSA_KV_HEADS):
            kg = k_ref[rs, g * DSA_DH:(g + 1) * DSA_DH]
            logit = _dot_nt(kg, q_rows[g]) + bias_w
            m_old = m_ref[g]
            m_new = jnp.maximum(m_old, jnp.max(logit, axis=0, keepdims=True))
            p = jnp.exp2(logit - m_new).astype(BF16)
            acc_ref[g] = acc_ref[g] * jnp.exp2(m_old - m_new) + _dot(vt_ref[g, i], p)
            m_ref[g] = m_new
        return carry
    lax.fori_loop(0, n_kc, attn_chunk, 0)

    for g in range(DSA_KV_HEADS):
        acc = acc_ref[g]
        ot = acc[0:DSA_DH] / acc[DSA_DH:DSA_DH + 1]
        for r in range(rep):
            hh = g * rep + r
            o_ref[:, hh * DSA_DH:(hh + 1) * DSA_DH] = ot[:, r * nq:(r + 1) * nq].T.astype(o_ref.dtype)


def _dsa(proj, small, ikg):
    b, s, _ = proj.shape
    nq = DSA_QB
    k_top = min(TOPK_MAX, s // 4)
    rep = DSA_HEADS // DSA_KV_HEADS
    return pl.pallas_call(
        functools.partial(_dsa_body, seq=s, k_top=k_top),
        grid=(b, s // nq),
        in_specs=[
            pl.BlockSpec((None, nq, DSA_HEADS * DSA_DH), lambda bi, qb: (bi, qb, COL_DQ // (DSA_HEADS * DSA_DH))),
            pl.BlockSpec((None, nq, IDX_HEADS * IDX_DIM), lambda bi, qb: (bi, qb, COL_IQ // (IDX_HEADS * IDX_DIM))),
            pl.BlockSpec((None, nq, LANES), lambda bi, qb: (bi, qb, 0)),
            pl.BlockSpec((None, s, DSA_KV_HEADS * DSA_DH), lambda bi, qb: (bi, 0, COL_DK // (DSA_KV_HEADS * DSA_DH))),
            pl.BlockSpec((None, s, DSA_KV_HEADS * DSA_DH), lambda bi, qb: (bi, 0, COL_DV // (DSA_KV_HEADS * DSA_DH))),
            pl.BlockSpec((None, s, LANES), lambda bi, qb: (bi, 0, 0)),
            pl.BlockSpec((1, LANES), lambda bi, qb: (0, 0)),
        ],
        out_specs=pl.BlockSpec((None, nq, DSA_HEADS * DSA_DH), lambda bi, qb: (bi, qb, 0)),
        out_shape=jax.ShapeDtypeStruct((b, s, DSA_HEADS * DSA_DH), BF16),
        scratch_shapes=[
            pltpu.VMEM((2, s, LANES), BF16),
            pltpu.VMEM((DSA_KV_HEADS, s // DSA_KC, DSA_DH + DSA_ONES, DSA_KC), BF16),
            pltpu.VMEM((s, nq), F32),
            pltpu.VMEM((s, nq), F32),
            pltpu.VMEM((1, nq), I32),
            pltpu.VMEM((DSA_KV_HEADS, 1, rep * nq), F32),
            pltpu.VMEM((DSA_KV_HEADS, DSA_DH + DSA_ONES, rep * nq), F32),
        ],
        compiler_params=_cparams(("arbitrary", "arbitrary")),
        name="dsa",
    )(proj, proj, small, proj, proj, small, ikg)


def _merge_body(x_ref, ga_ref, gb_ref, ba_ref, bb_ref, og_ref, od_ref, wo_ref, h_ref):
    sa = _sigmoid(ga_ref[...].astype(F32) + ba_ref[...])
    sb = _sigmoid(gb_ref[...].astype(F32) + bb_ref[...])
    mixed = sa * og_ref[...].astype(F32) + sb * od_ref[...].astype(F32)
    h_ref[...] = x_ref[...] + _dot(mixed.astype(BF16), wo_ref[...])


def _merge(x2, proj2, bias_a, bias_b, o_gdn2, o_dsa2, w_out, *, tm=512):
    n = x2.shape[0]
    const = lambda i: (0, 0)
    row = lambda i: (i, 0)
    return pl.pallas_call(
        _merge_body,
        grid=(n // tm,),
        in_specs=[
            pl.BlockSpec((tm, D_MODEL), row),
            pl.BlockSpec((tm, D_MODEL), lambda i: (i, COL_GA // D_MODEL)),
            pl.BlockSpec((tm, D_MODEL), lambda i: (i, COL_GB // D_MODEL)),
            pl.BlockSpec((1, D_MODEL), const),
            pl.BlockSpec((1, D_MODEL), const),
            pl.BlockSpec((tm, D_MODEL), row),
            pl.BlockSpec((tm, D_MODEL), row),
            pl.BlockSpec((D_MODEL, D_MODEL), const),
        ],
        out_specs=pl.BlockSpec((tm, D_MODEL), row),
        out_shape=jax.ShapeDtypeStruct((n, D_MODEL), F32),
        compiler_params=_cparams(("arbitrary",)),
        name="merge",
    )(x2, proj2, proj2, bias_a, bias_b, o_gdn2, o_dsa2, w_out)


FFN_CHUNK = 256


def _ffn_body(h_ref, g2_ref, wup_ref, cw_ref, cb_ref, wdn_ref, gf_ref, o_ref,
              carry_ref, xx_ref, *, tm):
    t = pl.program_id(1)

    @pl.when(t == 0)
    def _():
        carry_ref[...] = jnp.zeros_like(carry_ref)

    h = h_ref[...]
    hn = ((h * lax.rsqrt(jnp.mean(h * h, axis=-1, keepdims=True) + EPS)) * g2_ref[...]).astype(BF16)

    def conv_cols(c0):
        cs = slice(c0, c0 + FFN_CHUNK)
        up = _dot(hn, wup_ref[:, cs])
        xx_ref[0:SUBLANES, :] = carry_ref[:, cs]
        xx_ref[SUBLANES:SUBLANES + tm, :] = up
        carry_ref[:, cs] = up[tm - SUBLANES:tm, :]
        w = cw_ref[:, cs]
        y = w[FFN_CONV - 1:FFN_CONV, :] * up + cb_ref[:, cs]
        for j in range(FFN_CONV - 1):
            s0 = SUBLANES - (FFN_CONV - 1) + j
            y = y + w[j:j + 1, :] * xx_ref[s0:s0 + tm, :]
        return y

    acc = jnp.zeros((tm, D_MODEL), F32)
    for c in range(D_FF // FFN_CHUNK):
        gate = conv_cols(c * FFN_CHUNK)
        val = conv_cols(D_FF + c * FFN_CHUNK)
        act = (_silu(gate) * val).astype(BF16)
        acc = acc + _dot(act, wdn_ref[c * FFN_CHUNK:(c + 1) * FFN_CHUNK, :])
    h2 = h + acc
    o_ref[...] = (h2 * lax.rsqrt(jnp.mean(h2 * h2, axis=-1, keepdims=True) + EPS)) * gf_ref[...]


def _ffn(h1, g2, w_up, conv_w, conv_b, w_down, gf, *, tm=256):
    b, s, _ = h1.shape
    const = lambda bi, t: (0, 0)
    return pl.pallas_call(
        functools.partial(_ffn_body, tm=tm),
        grid=(b, s // tm),
        in_specs=[
            pl.BlockSpec((None, tm, D_MODEL), lambda bi, t: (bi, t, 0)),
            pl.BlockSpec((1, D_MODEL), const),
            pl.BlockSpec((D_MODEL, 2 * D_FF), const, pipeline_mode=pl.Buffered(1)),
            pl.BlockSpec((FFN_CONV, 2 * D_FF), const),
            pl.BlockSpec((1, 2 * D_FF), const),
            pl.BlockSpec((D_FF, D_MODEL), const, pipeline_mode=pl.Buffered(1)),
            pl.BlockSpec((1, D_MODEL), const),
        ],
        out_specs=pl.BlockSpec((None, tm, D_MODEL), lambda bi, t: (bi, t, 0)),
        out_shape=jax.ShapeDtypeStruct((b, s, D_MODEL), F32),
        scratch_shapes=[
            pltpu.VMEM((SUBLANES, 2 * D_FF), F32),
            pltpu.VMEM((SUBLANES + tm, FFN_CHUNK), F32),
        ],
        compiler_params=_cparams(("arbitrary", "arbitrary")),
        name="ffn",
    )(h1, g2, w_up, conv_w, conv_b, w_down, gf)


def _rearranged_w_in(w):
    o = np.cumsum([0, 3072, 1024, 8, 8, 1024, 256, 256, 512, 64, 8, 1024, 1024])
    g_qkv, g_z, g_a, g_b, d_q, d_k, d_v, i_q, i_k, i_w, gate_a, gate_b = (
        w[:, int(o[i]):int(o[i + 1])] for i in range(12))
    big = jnp.concatenate([g_qkv, g_z, d_q, d_k, d_v, i_q, gate_a, gate_b], axis=1).astype(BF16)
    pad = jnp.zeros((w.shape[0], LANES - (IDX_DIM + 3 * 8)), w.dtype)
    small = jnp.concatenate([i_k, g_a, g_b, i_w, pad], axis=1).astype(BF16)
    return big, small


def _lane_vec(v, offset):
    return jnp.zeros((1, LANES), F32).at[0, offset:offset + v.shape[0]].set(v.astype(F32))


def kernel(x, norm1_g, w_in, gdn_conv_w, gdn_a_log, gdn_dt_bias, gdn_norm_g, idx_k_norm_g,
           branch_gate_b, w_out, norm2_g, ffn_w_up, ffn_conv_w, ffn_conv_b, ffn_w_down, final_g):
    b, s, d = x.shape
    depth = norm1_g.shape[0]
    assert depth == 1, "the final RMSNorm is fused into the (single) channel-mixer call"
    h = x
    for l in range(depth):
        w_big, w_small = _rearranged_w_in(w_in[l])
        proj2, small2 = _inproj(h.reshape(b * s, d), norm1_g[l][None, :], w_big, w_small)
        proj = proj2.reshape(b, s, N_BIG)
        small = small2.reshape(b, s, LANES)
        o_gdn = _gdn(proj, small, gdn_conv_w[l], _lane_vec(gdn_a_log[l], SM_A),
                     _lane_vec(gdn_dt_bias[l], SM_A), gdn_norm_g[l][None, :].astype(F32))
        o_dsa = _dsa(proj, small, _lane_vec(idx_k_norm_g[l], SM_IK))
        gb = branch_gate_b[l]
        h1 = _merge(h.reshape(b * s, d), proj2, gb[None, :D_MODEL], gb[None, D_MODEL:],
                    o_gdn.reshape(b * s, -1), o_dsa.reshape(b * s, -1), w_out[l].astype(BF16))
        h = _ffn(h1.reshape(b, s, d), norm2_g[l][None, :], ffn_w_up[l].astype(BF16), ffn_conv_w[l],
                 ffn_conv_b[l][None, :], ffn_w_down[l].astype(BF16), final_g[None, :])
    return h
```

```python
import functools

import jax
import jax.numpy as jnp
import numpy as np
from jax import lax
from jax.experimental import pallas as pl
from jax.experimental.pallas import tpu as pltpu

F32 = jnp.float32
BF16 = jnp.bfloat16
I32 = jnp.int32

LANES = 128
SUBLANES = 8
VMEM_LIMIT = 56 * 1024 * 1024

D_MODEL = 1024
GDN_HEADS = 8
GDN_D = 128
GDN_CONV = 4
DSA_HEADS = 8
DSA_KV_HEADS = 2
DSA_DH = 128
IDX_HEADS = 8
IDX_DIM = 64
TOPK_MAX = 256
D_FF = 2816
FFN_CONV = 3
EPS = 1e-6

COL_QKV = 0
COL_Z = 3072
COL_DQ = 4096
COL_DK = 5120
COL_DV = 5376
COL_IQ = 5632
COL_GA = 6144
COL_GB = 7168
N_BIG = 8192
SM_IK = 0
SM_A = 64
SM_B = 72
SM_IW = 80

NEG_BIG = -1e30


def _cparams(sem):
    return pltpu.CompilerParams(dimension_semantics=sem, vmem_limit_bytes=VMEM_LIMIT)


def _dot(a, b):
    return jnp.dot(a, b, preferred_element_type=F32)


def _dot_nt(a, b):
    return lax.dot_general(a, b, (((1,), (1,)), ((), ())), preferred_element_type=F32)


def _dot_tn(a, b):
    return lax.dot_general(a, b, (((0,), (0,)), ((), ())), preferred_element_type=F32)


def _split3(x):
    h = x.astype(BF16)
    r = x - h.astype(F32)
    m = r.astype(BF16)
    l = (r - m.astype(F32)).astype(BF16)
    return h, m, l


def _sigmoid(x):
    return 1.0 / (1.0 + jnp.exp(-x))


def _silu(x):
    return x * _sigmoid(x)


def _softplus(x):
    return jnp.maximum(x, 0.0) + jnp.log(1.0 + jnp.exp(-jnp.abs(x)))


def _inproj_body(x_ref, g_ref, w_ref, ws_ref, proj_ref, small_ref, *, col_chunk):
    x = x_ref[...]
    ms = jnp.mean(x * x, axis=-1, keepdims=True)
    u = ((x * lax.rsqrt(ms + EPS)) * g_ref[...]).astype(BF16)
    for c in range(N_BIG // col_chunk):
        sl = slice(c * col_chunk, (c + 1) * col_chunk)
        proj_ref[:, sl] = _dot(u, w_ref[:, sl]).astype(BF16)
    small_ref[...] = _dot(u, ws_ref[...])


def _inproj(x2, g, w_big, w_small, *, tm=256):
    n = x2.shape[0]
    const = lambda i: (0, 0)
    return pl.pallas_call(
        functools.partial(_inproj_body, col_chunk=1024),
        grid=(n // tm,),
        in_specs=[
            pl.BlockSpec((tm, D_MODEL), lambda i: (i, 0)),
            pl.BlockSpec((1, D_MODEL), const),
            pl.BlockSpec((D_MODEL, N_BIG), const, pipeline_mode=pl.Buffered(1)),
            pl.BlockSpec((D_MODEL, LANES), const, pipeline_mode=pl.Buffered(1)),
        ],
        out_specs=[
            pl.BlockSpec((tm, N_BIG), lambda i: (i, 0)),
            pl.BlockSpec((tm, LANES), lambda i: (i, 0)),
        ],
        out_shape=[
            jax.ShapeDtypeStruct((n, N_BIG), BF16),
            jax.ShapeDtypeStruct((n, LANES), F32),
        ],
        compiler_params=_cparams(("arbitrary",)),
        name="inproj",
    )(x2, g, w_big, w_small)


GDN_C = 128


def _tri_inverse(a_list):
    n = a_list[0].shape[0]
    row = lax.broadcasted_iota(I32, (n, n), 0)
    col = lax.broadcasted_iota(I32, (n, n), 1)
    eye = jnp.where(row == col, 1.0, 0.0).astype(F32)

    def same_block(size):
        sh = size.bit_length() - 1
        return (row >> sh) == (col >> sh)

    def square(ps):
        return [_dot(p, p).astype(BF16) for p in ps]

    def grow(ts, ps):
        return [t + _dot(t.astype(BF16), p) for t, p in zip(ts, ps)]

    p1 = [jnp.where(same_block(16), a, 0.0) for a in a_list]
    p2s = square([p.astype(BF16) for p in p1])
    ts = [eye - p for p in p1]
    p4s = square(p2s)
    ts = grow(ts, p2s)
    p8s = square(p4s)
    ts = grow(ts, p4s)
    ts = grow(ts, p8s)
    size = 16
    while size < n:
        in_pair = same_block(2 * size) & jnp.logical_not(same_block(size))
        offs = [jnp.where(in_pair, a, 0.0).astype(BF16) for a in a_list]
        tss = [t.astype(BF16) for t in ts]
        mids = [_dot(t2, o).astype(BF16) for t2, o in zip(tss, offs)]
        ts = [t - _dot(m, t2) for t, m, t2 in zip(ts, mids, tss)]
        size *= 2
    return ts


def _gdn_body(qp_ref, kp_ref, vp_ref, z_ref, sm_ref, wq_ref, wk_ref, wv_ref,
              alog_ref, dtb_ref, ng_ref, o_ref,
              state_ref, halo_ref, xx_ref, *, tb, hg):
    hgi = pl.program_id(1)
    t = pl.program_id(2)

    @pl.when(t == 0)
    def _():
        state_ref[...] = jnp.zeros_like(state_ref)
        halo_ref[...] = jnp.zeros_like(halo_ref)

    def conv_silu(idx, x_ref, w_ref):
        x = x_ref[...].astype(F32)
        xx_ref[0:SUBLANES, :] = halo_ref[idx]
        xx_ref[SUBLANES:SUBLANES + tb, :] = x
        halo_ref[idx] = x[tb - SUBLANES:tb, :]
        w = w_ref[...]
        y = w[GDN_CONV - 1:GDN_CONV, :] * x
        for j in range(GDN_CONV - 1):
            s0 = SUBLANES - (GDN_CONV - 1) + j
            y = y + w[j:j + 1, :] * xx_ref[s0:s0 + tb, :]
        return _silu(y)

    q_all = conv_silu(0, qp_ref, wq_ref)
    k_all = conv_silu(1, kp_ref, wk_ref)
    v_all = conv_silu(2, vp_ref, wv_ref)

    sm = sm_ref[...]
    ld_all = -jnp.exp(alog_ref[...]) * _softplus(sm + dtb_ref[...])
    beta_all = _sigmoid(sm)

    c = GDN_C
    row = lax.broadcasted_iota(I32, (c, c), 0)
    col = lax.broadcasted_iota(I32, (c, c), 1)
    lane = lax.broadcasted_iota(I32, (c, LANES), 1)
    tri_incl = jnp.where(row >= col, 1.0, 0.0).astype(BF16)
    ng = ng_ref[...]

    for ci in range(tb // c):
        rs = slice(ci * c, (ci + 1) * c)
        l1, l2, l3 = _split3(ld_all[rs])
        g_all = _dot(tri_incl, l1) + (_dot(tri_incl, l2) + _dot(tri_incl, l3))
        heads = range(hg)
        cols = [slice(j * GDN_D, (j + 1) * GDN_D) for j in heads]
        gc, bc, gamma, eg, g_last, qc, kc, kb, kc16 = ([None] * hg for _ in range(9))
        for j in heads:
            h = hgi * hg + j
            gcol = jnp.sum(jnp.where(lane == SM_A + h, g_all, 0.0), axis=-1, keepdims=True)
            bc[j] = jnp.sum(jnp.where(lane == SM_B + h, beta_all[rs], 0.0), axis=-1, keepdims=True)
            gc[j] = jnp.broadcast_to(gcol, (c, c))
            gamma[j] = jnp.exp(jnp.where(row >= col, gc[j] - gc[j].T, -jnp.inf))
            eg[j] = jnp.exp(gc[j])
            g_last[j] = gc[j][c - 1:c, :]
            q_h = q_all[rs, cols[j]]
            k_h = k_all[rs, cols[j]]
            qc[j] = q_h * (lax.rsqrt(jnp.sum(q_h * q_h, axis=-1, keepdims=True) + EPS) * (GDN_D ** -0.5))
            kc[j] = k_h * lax.rsqrt(jnp.sum(k_h * k_h, axis=-1, keepdims=True) + EPS)
            kb[j] = kc[j] * bc[j]
            kc16[j] = kc[j].astype(BF16)
        a = [jnp.where(row > col, _dot_nt(kb[j].astype(BF16), kc16[j]) * gamma[j], 0.0) for j in heads]
        att = [(_dot_nt(qc[j].astype(BF16), kc16[j]) * gamma[j]).astype(BF16) for j in heads]
        tinv = [t.astype(BF16) for t in _tri_inverse(a)]
        u = [_dot(tinv[j], (v_all[rs, cols[j]] * bc[j]).astype(BF16)) for j in heads]
        w = [_dot(tinv[j], (kb[j] * eg[j]).astype(BF16)).astype(BF16) for j in heads]
        qg = [(qc[j] * eg[j]).astype(BF16) for j in heads]
        kd = [(kc[j] * jnp.exp(g_last[j] - gc[j])).astype(BF16) for j in heads]

        s_old = [state_ref[j] for j in heads]
        s16 = [s.astype(BF16) for s in s_old]
        v_new16 = [(u[j] - _dot(w[j], s16[j])).astype(BF16) for j in heads]
        o = [_dot(qg[j], s16[j]) + _dot(att[j], v_new16[j]) for j in heads]
        for j in heads:
            state_ref[j] = s_old[j] * jnp.exp(g_last[j]) + _dot_tn(kd[j], v_new16[j])
        for j in heads:
            y = o[j] * lax.rsqrt(jnp.mean(o[j] * o[j], axis=-1, keepdims=True) + EPS) * ng
            o_ref[rs, cols[j]] = (y * _silu(z_ref[rs, cols[j]].astype(F32))).astype(o_ref.dtype)


def _gdn(proj, small, conv_w, alog_v, dtb_v, ng, *, tb=128, hg=8):
    b, s, _ = proj.shape
    nh = GDN_HEADS
    w = hg * GDN_D
    ng_blocks = nh // hg
    hq = COL_QKV // w
    hz = COL_Z // w
    return pl.pallas_call(
        functools.partial(_gdn_body, tb=tb, hg=hg),
        grid=(b, ng_blocks, s // tb),
        in_specs=[
            pl.BlockSpec((None, tb, w), lambda bi, g, t: (bi, t, hq + g)),
            pl.BlockSpec((None, tb, w), lambda bi, g, t: (bi, t, hq + ng_blocks + g)),
            pl.BlockSpec((None, tb, w), lambda bi, g, t: (bi, t, hq + 2 * ng_blocks + g)),
            pl.BlockSpec((None, tb, w), lambda bi, g, t: (bi, t, hz + g)),
            pl.BlockSpec((None, tb, LANES), lambda bi, g, t: (bi, t, 0)),
            pl.BlockSpec((GDN_CONV, w), lambda bi, g, t: (0, g)),
            pl.BlockSpec((GDN_CONV, w), lambda bi, g, t: (0, ng_blocks + g)),
            pl.BlockSpec((GDN_CONV, w), lambda bi, g, t: (0, 2 * ng_blocks + g)),
            pl.BlockSpec((1, LANES), lambda bi, g, t: (0, 0)),
            pl.BlockSpec((1, LANES), lambda bi, g, t: (0, 0)),
            pl.BlockSpec((1, LANES), lambda bi, g, t: (0, 0)),
        ],
        out_specs=pl.BlockSpec((None, tb, w), lambda bi, g, t: (bi, t, g)),
        out_shape=jax.ShapeDtypeStruct((b, s, nh * GDN_D), BF16),
        scratch_shapes=[
            pltpu.VMEM((hg, GDN_D, GDN_D), F32),
            pltpu.VMEM((3, SUBLANES, w), F32),
            pltpu.VMEM((SUBLANES + tb, w), F32),
        ],
        compiler_params=_cparams(("arbitrary", "arbitrary", "arbitrary")),
        name="gdn",
    )(proj, proj, proj, proj, small, conv_w, conv_w, conv_w, alog_v, dtb_v, ng)


DSA_QB = 128
DSA_KC = 512
DSA_AC = 512
DSA_ONES = 16
_KEY_NEG_INF = np.int32(np.array(0xFF800000, np.uint32).view(np.int32) ^ np.int32(0x7FFFFFFF))
_KEY_POS_INF_P1 = np.int32(0x7F800001)


def _key_to_f32(key):
    bits = key ^ ((key >> 31) & jnp.int32(0x7FFFFFFF))
    return lax.bitcast_convert_type(bits, F32)


def _tree_sum(xs):
    xs = list(xs)
    while len(xs) > 1:
        nxt = [xs[i] + xs[i + 1] for i in range(0, len(xs) - 1, 2)]
        if len(xs) % 2:
            nxt.append(xs[-1])
        xs = nxt
    return xs[0]


def _count_rows(hit):
    rows, n = hit.shape
    parts = hit.reshape(rows // SUBLANES, SUBLANES, n)
    return _tree_sum([parts[i] for i in range(rows // SUBLANES)])


def _dsa_body(q_ref, iq_ref, smq_ref, k_ref, v_ref, smk_ref, ikg_ref, o_ref,
              ikn_ref, vt_ref, s_ref, bias_ref, jsel_ref, m_ref, alpha_ref, acc_ref, lg_ref, *, seq, k_top):
    qb = pl.program_id(1)
    kc = DSA_KC
    ac = DSA_AC
    nq = DSA_QB
    rep = DSA_HEADS // DSA_KV_HEADS

    @pl.when(qb == 0)
    def _():
        def prep(i, carry):
            rs = pl.ds(pl.multiple_of(i * kc, kc), kc)
            sm = smk_ref[rs, :]
            lane = lax.broadcasted_iota(I32, sm.shape, 1)
            isk = lane < IDX_DIM
            mean = jnp.sum(jnp.where(isk, sm, 0.0), axis=-1, keepdims=True) * (1.0 / IDX_DIM)
            xc = jnp.where(isk, sm - mean, 0.0)
            var = jnp.sum(xc * xc, axis=-1, keepdims=True) * (1.0 / IDX_DIM)
            y = xc * lax.rsqrt(var + EPS) * ikg_ref[...]
            ikn_ref[0, rs, :] = y.astype(BF16)
            ikn_ref[1, rs, :] = pltpu.roll(y, IDX_DIM, axis=1).astype(BF16)
            vv = v_ref[rs, :].astype(F32)
            for g in range(DSA_KV_HEADS):
                vt = vv[:, g * DSA_DH:(g + 1) * DSA_DH].T.astype(BF16)
                for a in range(kc // ac):
                    vt_ref[g, i * (kc // ac) + a, 0:DSA_DH, :] = vt[:, a * ac:(a + 1) * ac]
                    vt_ref[g, i * (kc // ac) + a, DSA_DH:DSA_DH + DSA_ONES, :] = jnp.ones((DSA_ONES, ac), BF16)
            return carry
        lax.fori_loop(0, seq // kc, prep, 0)

    q_lo = qb * nq
    n_kc = (q_lo + nq + kc - 1) // kc
    qpos = q_lo + lax.broadcasted_iota(I32, (1, nq), 1)

    iq = iq_ref[...]
    n_pair = IDX_HEADS // 2
    iq_rows = jnp.concatenate([iq[:, p * LANES:(p + 1) * LANES] for p in range(n_pair)], axis=0)
    iw_t = smq_ref[...].T * ((IDX_HEADS ** -0.5) * (IDX_DIM ** -0.5))

    def score_chunk(i, carry):
        rs = pl.ds(pl.multiple_of(i * kc, kc), kc)
        lo = _dot_nt(ikn_ref[0, rs, :], iq_rows)
        hi = _dot_nt(ikn_ref[1, rs, :], iq_rows)
        sc = jnp.zeros((kc, nq), F32)
        for p in range(n_pair):
            cs = slice(p * nq, (p + 1) * nq)
            sc = sc + iw_t[SM_IW + 2 * p:SM_IW + 2 * p + 1, :] * jnp.maximum(lo[:, cs], 0.0)
            sc = sc + iw_t[SM_IW + 2 * p + 1:SM_IW + 2 * p + 2, :] * jnp.maximum(hi[:, cs], 0.0)
        kpos = i * kc + lax.broadcasted_iota(I32, (kc, nq), 0)
        s_ref[rs, :] = jnp.where(kpos <= qpos, sc + 0.0, -jnp.inf)
        return carry
    lax.fori_loop(0, n_kc, score_chunk, 0)

    def count_ge(thr):
        def body(i, cnt):
            rs = pl.ds(pl.multiple_of(i * kc, kc), kc)
            return cnt + _count_rows(jnp.where(s_ref[rs, :] >= thr, 1.0, 0.0))
        cnt = lax.fori_loop(0, n_kc, body, jnp.zeros((SUBLANES, nq), F32))
        return jnp.sum(cnt, axis=0, keepdims=True)

    kf = jnp.float32(k_top)

    def midpoint(lo, hi):
        return (lo >> 1) + (hi >> 1) + (lo & hi & 1)

    def bisect(_, carry):
        lo, hi = carry
        mid = midpoint(lo, hi)
        ok = count_ge(_key_to_f32(mid)) >= kf
        return jnp.where(ok, mid, lo), jnp.where(ok, hi, mid)
    lo0 = jnp.full((1, nq), _KEY_NEG_INF, I32)
    hi0 = jnp.full((1, nq), _KEY_POS_INF_P1, I32)
    lo, _ = lax.fori_loop(0, 32, bisect, (lo0, hi0))
    thr = _key_to_f32(lo)

    def count_gt_ge(i, carry):
        c_gt, c_ge = carry
        rs = pl.ds(pl.multiple_of(i * kc, kc), kc)
        s = s_ref[rs, :]
        return (c_gt + _count_rows(jnp.where(s > thr, 1.0, 0.0)),
                c_ge + _count_rows(jnp.where(s >= thr, 1.0, 0.0)))
    z8 = jnp.zeros((SUBLANES, nq), F32)
    c_gt, c_ge = lax.fori_loop(0, n_kc, count_gt_ge, (z8, z8))
    n_gt = jnp.sum(c_gt, axis=0, keepdims=True)
    n_ge = jnp.sum(c_ge, axis=0, keepdims=True)
    need = kf - n_gt
    jsel_ref[...] = jnp.full((1, nq), seq, I32)

    @pl.when(jnp.max(n_ge) > kf)
    def _():
        def count_eq_upto(jmax):
            def body(i, cnt):
                rs = pl.ds(pl.multiple_of(i * kc, kc), kc)
                kpos = i * kc + lax.broadcasted_iota(I32, (kc, nq), 0)
                hit = jnp.where(s_ref[rs, :] == thr, jnp.where(kpos <= jmax, 1.0, 0.0), 0.0)
                return cnt + _count_rows(hit)
            cnt = lax.fori_loop(0, n_kc, body, jnp.zeros((SUBLANES, nq), F32))
            return jnp.sum(cnt, axis=0, keepdims=True)

        def jbisect(_, carry):
            lo_j, hi_j = carry
            mid = (lo_j + hi_j) >> 1
            ok = count_eq_upto(mid) >= need
            return jnp.where(ok, lo_j, mid), jnp.where(ok, mid, hi_j)
        n_iter = int(np.ceil(np.log2(seq))) + 1
        lo_j0 = jnp.full((1, nq), -1, I32)
        hi_j0 = jnp.full((1, nq), seq - 1, I32)
        _, hi_j = lax.fori_loop(0, n_iter, jbisect, (lo_j0, hi_j0))
        jsel_ref[...] = hi_j

    jsel = jsel_ref[...]

    def bias_chunk(i, carry):
        rs = pl.ds(pl.multiple_of(i * kc, kc), kc)
        s = s_ref[rs, :]
        kpos = i * kc + lax.broadcasted_iota(I32, (kc, nq), 0)
        tie = jnp.where(s == thr, jnp.where(kpos <= jsel, 0.0, NEG_BIG), NEG_BIG)
        sel = jnp.where(s > thr, 0.0, tie)
        bias_ref[rs, :] = jnp.where(kpos <= qpos, sel, NEG_BIG)
        return carry
    lax.fori_loop(0, n_kc, bias_chunk, 0)

    qall = (q_ref[...].astype(F32) * (DSA_DH ** -0.5 * np.log2(np.e))).astype(BF16)
    q_rows = [jnp.concatenate([qall[:, (g * rep + r) * DSA_DH:(g * rep + r + 1) * DSA_DH]
                               for r in range(rep)], axis=0) for g in range(DSA_KV_HEADS)]
    m_ref[...] = jnp.full(m_ref.shape, NEG_BIG, F32)
    acc_ref[...] = jnp.zeros(acc_ref.shape, F32)

    def logits_phase(i):
        rs = pl.ds(pl.multiple_of(i * ac, ac), ac)
        bias = bias_ref[rs, :]
        bias_w = jnp.concatenate([bias] * rep, axis=1)
        for g in range(DSA_KV_HEADS):
            logit = _dot_nt(k_ref[rs, g * DSA_DH:(g + 1) * DSA_DH], q_rows[g]) + bias_w
            lg_ref[g] = logit
            m_old = m_ref[g]
            m_new = jnp.maximum(m_old, jnp.max(logit, axis=0, keepdims=True))
            alpha_ref[g] = jnp.exp2(m_old - m_new)
            m_ref[g] = m_new

    def value_phase(i):
        for g in range(DSA_KV_HEADS):
            p = jnp.exp2(lg_ref[g] - m_ref[g]).astype(BF16)
            acc_ref[g] = acc_ref[g] * alpha_ref[g] + _dot(vt_ref[g, i], p)

    logits_phase(0)

    def attn_chunk(i, carry):
        value_phase(i - 1)
        logits_phase(i)
        return carry
    n_ac = (q_lo + nq + ac - 1) // ac
    lax.fori_loop(1, n_ac, attn_chunk, 0)
    value_phase(n_ac - 1)

    for g in range(DSA_KV_HEADS):
        acc = acc_ref[g]
        ot = acc[0:DSA_DH] / acc[DSA_DH:DSA_DH + 1]
        for r in range(rep):
            hh = g * rep + r
            o_ref[:, hh * DSA_DH:(hh + 1) * DSA_DH] = ot[:, r * nq:(r + 1) * nq].T.astype(o_ref.dtype)


def _dsa(proj, small, ikg):
    b, s, _ = proj.shape
    nq = DSA_QB
    k_top = min(TOPK_MAX, s // 4)
    rep = DSA_HEADS // DSA_KV_HEADS
    return pl.pallas_call(
        functools.partial(_dsa_body, seq=s, k_top=k_top),
        grid=(b, s // nq),
        in_specs=[
            pl.BlockSpec((None, nq, DSA_HEADS * DSA_DH), lambda bi, qb: (bi, qb, COL_DQ // (DSA_HEADS * DSA_DH))),
            pl.BlockSpec((None, nq, IDX_HEADS * IDX_DIM), lambda bi, qb: (bi, qb, COL_IQ // (IDX_HEADS * IDX_DIM))),
            pl.BlockSpec((None, nq, LANES), lambda bi, qb: (bi, qb, 0)),
            pl.BlockSpec((None, s, DSA_KV_HEADS * DSA_DH), lambda bi, qb: (bi, 0, COL_DK // (DSA_KV_HEADS * DSA_DH))),
            pl.BlockSpec((None, s, DSA_KV_HEADS * DSA_DH), lambda bi, qb: (bi, 0, COL_DV // (DSA_KV_HEADS * DSA_DH))),
            pl.BlockSpec((None, s, LANES), lambda bi, qb: (bi, 0, 0)),
            pl.BlockSpec((1, LANES), lambda bi, qb: (0, 0)),
        ],
        out_specs=pl.BlockSpec((None, nq, DSA_HEADS * DSA_DH), lambda bi, qb: (bi, qb, 0)),
        out_shape=jax.ShapeDtypeStruct((b, s, DSA_HEADS * DSA_DH), BF16),
        scratch_shapes=[
            pltpu.VMEM((2, s, LANES), BF16),
            pltpu.VMEM((DSA_KV_HEADS, s // DSA_AC, DSA_DH + DSA_ONES, DSA_AC), BF16),
            pltpu.VMEM((s, nq), F32),
            pltpu.VMEM((s, nq), F32),
            pltpu.VMEM((1, nq), I32),
            pltpu.VMEM((DSA_KV_HEADS, 1, rep * nq), F32),
            pltpu.VMEM((DSA_KV_HEADS, 1, rep * nq), F32),
            pltpu.VMEM((DSA_KV_HEADS, DSA_DH + DSA_ONES, rep * nq), F32),
            pltpu.VMEM((DSA_KV_HEADS, DSA_AC, rep * nq), F32),
        ],
        compiler_params=_cparams(("arbitrary", "arbitrary")),
        name="dsa",
    )(proj, proj, small, proj, proj, small, ikg)


def _merge_body(x_ref, ga_ref, gb_ref, ba_ref, bb_ref, og_ref, od_ref, wo_ref, h_ref):
    sa = _sigmoid(ga_ref[...].astype(F32) + ba_ref[...])
    sb = _sigmoid(gb_ref[...].astype(F32) + bb_ref[...])
    mixed = sa * og_ref[...].astype(F32) + sb * od_ref[...].astype(F32)
    h_ref[...] = x_ref[...] + _dot(mixed.astype(BF16), wo_ref[...])


def _merge(x2, proj2, bias_a, bias_b, o_gdn2, o_dsa2, w_out, *, tm=512):
    n = x2.shape[0]
    const = lambda i: (0, 0)
    row = lambda i: (i, 0)
    return pl.pallas_call(
        _merge_body,
        grid=(n // tm,),
        in_specs=[
            pl.BlockSpec((tm, D_MODEL), row),
            pl.BlockSpec((tm, D_MODEL), lambda i: (i, COL_GA // D_MODEL)),
            pl.BlockSpec((tm, D_MODEL), lambda i: (i, COL_GB // D_MODEL)),
            pl.BlockSpec((1, D_MODEL), const),
            pl.BlockSpec((1, D_MODEL), const),
            pl.BlockSpec((tm, D_MODEL), row),
            pl.BlockSpec((tm, D_MODEL), row),
            pl.BlockSpec((D_MODEL, D_MODEL), const),
        ],
        out_specs=pl.BlockSpec((tm, D_MODEL), row),
        out_shape=jax.ShapeDtypeStruct((n, D_MODEL), F32),
        compiler_params=_cparams(("arbitrary",)),
        name="merge",
    )(x2, proj2, proj2, bias_a, bias_b, o_gdn2, o_dsa2, w_out)


FFN_CHUNK = 256


def _ffn_body(h_ref, g2_ref, wup_ref, cw_ref, cb_ref, wdn_ref, gf_ref, o_ref,
              carry_ref, xx_ref, *, tm):
    t = pl.program_id(1)

    @pl.when(t == 0)
    def _():
        carry_ref[...] = jnp.zeros_like(carry_ref)

    h = h_ref[...]
    hn = ((h * lax.rsqrt(jnp.mean(h * h, axis=-1, keepdims=True) + EPS)) * g2_ref[...]).astype(BF16)

    def up_cols(c0):
        cs = slice(c0, c0 + FFN_CHUNK)
        up = _dot(hn, wup_ref[:, cs])
        xx_ref[0:SUBLANES, cs] = carry_ref[:, cs]
        xx_ref[SUBLANES:SUBLANES + tm, cs] = up
        carry_ref[:, cs] = up[tm - SUBLANES:tm, :]

    def conv_cols(c0):
        cs = slice(c0, c0 + FFN_CHUNK)
        w = cw_ref[:, cs]
        y = cb_ref[:, cs] + w[0:1, :] * xx_ref[SUBLANES - (FFN_CONV - 1):SUBLANES - (FFN_CONV - 1) + tm, cs]
        for j in range(1, FFN_CONV):
            s0 = SUBLANES - (FFN_CONV - 1) + j
            y = y + w[j:j + 1, :] * xx_ref[s0:s0 + tm, cs]
        return y

    n_chunk = D_FF // FFN_CHUNK
    up_cols(0)
    up_cols(D_FF)
    acc = jnp.zeros((tm, D_MODEL), F32)
    for c in range(n_chunk):
        if c + 1 < n_chunk:
            up_cols((c + 1) * FFN_CHUNK)
            up_cols(D_FF + (c + 1) * FFN_CHUNK)
        gate = conv_cols(c * FFN_CHUNK)
        val = conv_cols(D_FF + c * FFN_CHUNK)
        act = (_silu(gate) * val).astype(BF16)
        acc = acc + _dot(act, wdn_ref[c * FFN_CHUNK:(c + 1) * FFN_CHUNK, :])
    h2 = h + acc
    o_ref[...] = (h2 * lax.rsqrt(jnp.mean(h2 * h2, axis=-1, keepdims=True) + EPS)) * gf_ref[...]


def _ffn(h1, g2, w_up, conv_w, conv_b, w_down, gf, *, tm=256):
    b, s, _ = h1.shape
    const = lambda bi, t: (0, 0)
    return pl.pallas_call(
        functools.partial(_ffn_body, tm=tm),
        grid=(b, s // tm),
        in_specs=[
            pl.BlockSpec((None, tm, D_MODEL), lambda bi, t: (bi, t, 0)),
            pl.BlockSpec((1, D_MODEL), const),
            pl.BlockSpec((D_MODEL, 2 * D_FF), const, pipeline_mode=pl.Buffered(1)),
            pl.BlockSpec((FFN_CONV, 2 * D_FF), const),
            pl.BlockSpec((1, 2 * D_FF), const),
            pl.BlockSpec((D_FF, D_MODEL), const, pipeline_mode=pl.Buffered(1)),
            pl.BlockSpec((1, D_MODEL), const),
        ],
        out_specs=pl.BlockSpec((None, tm, D_MODEL), lambda bi, t: (bi, t, 0)),
        out_shape=jax.ShapeDtypeStruct((b, s, D_MODEL), F32),
        scratch_shapes=[
            pltpu.VMEM((SUBLANES, 2 * D_FF), F32),
            pltpu.VMEM((SUBLANES + tm, 2 * D_FF), F32),
        ],
        compiler_params=_cparams(("arbitrary", "arbitrary")),
        name="ffn",
    )(h1, g2, w_up, conv_w, conv_b, w_down, gf)


def _rearranged_w_in(w):
    o = np.cumsum([0, 3072, 1024, 8, 8, 1024, 256, 256, 512, 64, 8, 1024, 1024])
    g_qkv, g_z, g_a, g_b, d_q, d_k, d_v, i_q, i_k, i_w, gate_a, gate_b = (
        w[:, int(o[i]):int(o[i + 1])] for i in range(12))
    big = jnp.concatenate([g_qkv, g_z, d_q, d_k, d_v, i_q, gate_a, gate_b], axis=1).astype(BF16)
    pad = jnp.zeros((w.shape[0], LANES - (IDX_DIM + 3 * 8)), w.dtype)
    small = jnp.concatenate([i_k, g_a, g_b, i_w, pad], axis=1).astype(BF16)
    return big, small


def _lane_vec(v, offset):
    return jnp.zeros((1, LANES), F32).at[0, offset:offset + v.shape[0]].set(v.astype(F32))


def kernel(x, norm1_g, w_in, gdn_conv_w, gdn_a_log, gdn_dt_bias, gdn_norm_g, idx_k_norm_g,
           branch_gate_b, w_out, norm2_g, ffn_w_up, ffn_conv_w, ffn_conv_b, ffn_w_down, final_g):
    b, s, d = x.shape
    depth = norm1_g.shape[0]
    assert depth == 1, "the final RMSNorm is fused into the (single) channel-mixer call"
    h = x
    for l in range(depth):
        w_big, w_small = _rearranged_w_in(w_in[l])
        proj2, small2 = _inproj(h.reshape(b * s, d), norm1_g[l][None, :], w_big, w_small)
        proj = proj2.reshape(b, s, N_BIG)
        small = small2.reshape(b, s, LANES)
        o_gdn = _gdn(proj, small, gdn_conv_w[l], _lane_vec(gdn_a_log[l], SM_A),
                     _lane_vec(gdn_dt_bias[l], SM_A), gdn_norm_g[l][None, :].astype(F32))
        o_dsa = _dsa(proj, small, _lane_vec(idx_k_norm_g[l], SM_IK))
        gb = branch_gate_b[l]
        h1 = _merge(h.reshape(b * s, d), proj2, gb[None, :D_MODEL], gb[None, D_MODEL:],
                    o_gdn.reshape(b * s, -1), o_dsa.reshape(b * s, -1), w_out[l].astype(BF16))
        h = _ffn(h1.reshape(b, s, d), norm2_g[l][None, :], ffn_w_up[l].astype(BF16), ffn_conv_w[l],
                 ffn_conv_b[l][None, :], ffn_w_down[l].astype(BF16), final_g[None, :])
    return h
```

```python
import functools

import jax
import jax.numpy as jnp
import numpy as np
from jax import lax
from jax.experimental import pallas as pl
from jax.experimental.pallas import tpu as pltpu

F32 = jnp.float32
BF16 = jnp.bfloat16
I32 = jnp.int32

LANES = 128
SUBLANES = 8
VMEM_LIMIT = 56 * 1024 * 1024

D_MODEL = 1024
GDN_HEADS = 8
GDN_D = 128
GDN_CONV = 4
DSA_HEADS = 8
DSA_KV_HEADS = 2
DSA_DH = 128
IDX_HEADS = 8
IDX_DIM = 64
TOPK_MAX = 256
D_FF = 2816
FFN_CONV = 3
EPS = 1e-6

COL_QKV = 0
COL_Z = 3072
COL_DQ = 4096
COL_DK = 5120
COL_DV = 5376
COL_IQ = 5632
COL_GA = 6144
COL_GB = 7168
N_BIG = 8192
SM_IK = 0
SM_A = 64
SM_B = 72
SM_IW = 80

NEG_BIG = -1e30


def _cparams(sem):
    return pltpu.CompilerParams(dimension_semantics=sem, vmem_limit_bytes=VMEM_LIMIT)


def _dot(a, b):
    return jnp.dot(a, b, preferred_element_type=F32)


def _dot_nt(a, b):
    return lax.dot_general(a, b, (((1,), (1,)), ((), ())), preferred_element_type=F32)


def _dot_tn(a, b):
    return lax.dot_general(a, b, (((0,), (0,)), ((), ())), preferred_element_type=F32)


def _split3(x):
    h = x.astype(BF16)
    r = x - h.astype(F32)
    m = r.astype(BF16)
    l = (r - m.astype(F32)).astype(BF16)
    return h, m, l


def _sigmoid(x):
    return 1.0 / (1.0 + jnp.exp(-x))


def _silu(x):
    h = 0.5 * x
    return h + h * jnp.tanh(h)


def _softplus(x):
    return jnp.maximum(x, 0.0) + jnp.log(1.0 + jnp.exp(-jnp.abs(x)))


def _inproj_body(x_ref, g_ref, w_ref, ws_ref, proj_ref, small_ref, *, col_chunk):
    x = x_ref[...]
    ms = jnp.mean(x * x, axis=-1, keepdims=True)
    u = ((x * lax.rsqrt(ms + EPS)) * g_ref[...]).astype(BF16)
    for c in range(N_BIG // col_chunk):
        sl = slice(c * col_chunk, (c + 1) * col_chunk)
        proj_ref[:, sl] = _dot(u, w_ref[:, sl]).astype(BF16)
    small_ref[...] = _dot(u, ws_ref[...])


def _inproj(x2, g, w_big, w_small, *, tm=256):
    n = x2.shape[0]
    const = lambda i: (0, 0)
    return pl.pallas_call(
        functools.partial(_inproj_body, col_chunk=1024),
        grid=(n // tm,),
        in_specs=[
            pl.BlockSpec((tm, D_MODEL), lambda i: (i, 0)),
            pl.BlockSpec((1, D_MODEL), const),
            pl.BlockSpec((D_MODEL, N_BIG), const, pipeline_mode=pl.Buffered(1)),
            pl.BlockSpec((D_MODEL, LANES), const, pipeline_mode=pl.Buffered(1)),
        ],
        out_specs=[
            pl.BlockSpec((tm, N_BIG), lambda i: (i, 0)),
            pl.BlockSpec((tm, LANES), lambda i: (i, 0)),
        ],
        out_shape=[
            jax.ShapeDtypeStruct((n, N_BIG), BF16),
            jax.ShapeDtypeStruct((n, LANES), F32),
        ],
        compiler_params=_cparams(("arbitrary",)),
        name="inproj",
    )(x2, g, w_big, w_small)


GDN_C = 128


def _tri_inverse(a_list):
    n = a_list[0].shape[0]
    row = lax.broadcasted_iota(I32, (n, n), 0)
    col = lax.broadcasted_iota(I32, (n, n), 1)
    eye = jnp.where(row == col, 1.0, 0.0).astype(F32)

    def same_block(size):
        sh = size.bit_length() - 1
        return (row >> sh) == (col >> sh)

    def square(ps):
        return [_dot(p, p).astype(BF16) for p in ps]

    def grow(ts, ps):
        return [t + _dot(t.astype(BF16), p) for t, p in zip(ts, ps)]

    p1 = [jnp.where(same_block(16), a, 0.0) for a in a_list]
    p2s = square([p.astype(BF16) for p in p1])
    ts = [eye - p for p in p1]
    p4s = square(p2s)
    ts = grow(ts, p2s)
    p8s = square(p4s)
    ts = grow(ts, p4s)
    ts = grow(ts, p8s)
    size = 16
    while size < n:
        in_pair = same_block(2 * size) & jnp.logical_not(same_block(size))
        offs = [jnp.where(in_pair, a, 0.0).astype(BF16) for a in a_list]
        tss = [t.astype(BF16) for t in ts]
        mids = [_dot(t2, o).astype(BF16) for t2, o in zip(tss, offs)]
        ts = [t - _dot(m, t2) for t, m, t2 in zip(ts, mids, tss)]
        size *= 2
    return ts


def _gdn_body(qp_ref, kp_ref, vp_ref, z_ref, sm_ref, wq_ref, wk_ref, wv_ref,
              alog_ref, dtb_ref, ng_ref, o_ref,
              state_ref, halo_ref, xx_ref, *, tb, hg):
    hgi = pl.program_id(1)
    t = pl.program_id(2)

    @pl.when(t == 0)
    def _():
        state_ref[...] = jnp.zeros_like(state_ref)
        halo_ref[...] = jnp.zeros_like(halo_ref)

    def conv_silu(idx, x_ref, w_ref):
        x = x_ref[...].astype(F32)
        xx_ref[0:SUBLANES, :] = halo_ref[idx]
        xx_ref[SUBLANES:SUBLANES + tb, :] = x
        halo_ref[idx] = x[tb - SUBLANES:tb, :]
        w = w_ref[...]
        y = w[GDN_CONV - 1:GDN_CONV, :] * x
        for j in range(GDN_CONV - 1):
            s0 = SUBLANES - (GDN_CONV - 1) + j
            y = y + w[j:j + 1, :] * xx_ref[s0:s0 + tb, :]
        return _silu(y)

    q_all = conv_silu(0, qp_ref, wq_ref)
    k_all = conv_silu(1, kp_ref, wk_ref)
    v_all = conv_silu(2, vp_ref, wv_ref)

    sm = sm_ref[...]
    ld_all = -jnp.exp(alog_ref[...]) * _softplus(sm + dtb_ref[...])
    beta_all = _sigmoid(sm)

    c = GDN_C
    row = lax.broadcasted_iota(I32, (c, c), 0)
    col = lax.broadcasted_iota(I32, (c, c), 1)
    lane = lax.broadcasted_iota(I32, (c, LANES), 1)
    tri_incl = jnp.where(row >= col, 1.0, 0.0).astype(BF16)
    ng = ng_ref[...]

    for ci in range(tb // c):
        rs = slice(ci * c, (ci + 1) * c)
        l1, l2, l3 = _split3(ld_all[rs])
        g_all = _dot(tri_incl, l1) + (_dot(tri_incl, l2) + _dot(tri_incl, l3))
        heads = range(hg)
        cols = [slice(j * GDN_D, (j + 1) * GDN_D) for j in heads]
        gc, bc, gamma, eg, g_last, qc, kc, kb, kc16 = ([None] * hg for _ in range(9))
        for j in heads:
            h = hgi * hg + j
            gcol = jnp.sum(jnp.where(lane == SM_A + h, g_all, 0.0), axis=-1, keepdims=True)
            bc[j] = jnp.sum(jnp.where(lane == SM_B + h, beta_all[rs], 0.0), axis=-1, keepdims=True)
            gc[j] = jnp.broadcast_to(gcol, (c, c))
            gamma[j] = jnp.exp(jnp.where(row >= col, gc[j] - gc[j].T, -jnp.inf))
            eg[j] = jnp.exp(gc[j])
            g_last[j] = gc[j][c - 1:c, :]
            q_h = q_all[rs, cols[j]]
            k_h = k_all[rs, cols[j]]
            qc[j] = q_h * (lax.rsqrt(jnp.sum(q_h * q_h, axis=-1, keepdims=True) + EPS) * (GDN_D ** -0.5))
            kc[j] = k_h * lax.rsqrt(jnp.sum(k_h * k_h, axis=-1, keepdims=True) + EPS)
            kb[j] = kc[j] * bc[j]
            kc16[j] = kc[j].astype(BF16)
        a = [jnp.where(row > col, _dot_nt(kb[j].astype(BF16), kc16[j]) * gamma[j], 0.0) for j in heads]
        att = [(_dot_nt(qc[j].astype(BF16), kc16[j]) * gamma[j]).astype(BF16) for j in heads]
        tinv = [t.astype(BF16) for t in _tri_inverse(a)]
        u = [_dot(tinv[j], (v_all[rs, cols[j]] * bc[j]).astype(BF16)) for j in heads]
        w = [_dot(tinv[j], (kb[j] * eg[j]).astype(BF16)).astype(BF16) for j in heads]
        qg = [(qc[j] * eg[j]).astype(BF16) for j in heads]
        kd = [(kc[j] * jnp.exp(g_last[j] - gc[j])).astype(BF16) for j in heads]

        s_old = [state_ref[j] for j in heads]
        s16 = [s.astype(BF16) for s in s_old]
        v_new16 = [(u[j] - _dot(w[j], s16[j])).astype(BF16) for j in heads]
        o = [_dot(qg[j], s16[j]) + _dot(att[j], v_new16[j]) for j in heads]
        for j in heads:
            state_ref[j] = s_old[j] * jnp.exp(g_last[j]) + _dot_tn(kd[j], v_new16[j])
        for j in heads:
            y = o[j] * lax.rsqrt(jnp.mean(o[j] * o[j], axis=-1, keepdims=True) + EPS) * ng
            o_ref[rs, cols[j]] = (y * _silu(z_ref[rs, cols[j]].astype(F32))).astype(o_ref.dtype)


def _gdn(proj, small, conv_w, alog_v, dtb_v, ng, *, tb=128, hg=8):
    b, s, _ = proj.shape
    nh = GDN_HEADS
    w = hg * GDN_D
    ng_blocks = nh // hg
    hq = COL_QKV // w
    hz = COL_Z // w
    return pl.pallas_call(
        functools.partial(_gdn_body, tb=tb, hg=hg),
        grid=(b, ng_blocks, s // tb),
        in_specs=[
            pl.BlockSpec((None, tb, w), lambda bi, g, t: (bi, t, hq + g)),
            pl.BlockSpec((None, tb, w), lambda bi, g, t: (bi, t, hq + ng_blocks + g)),
            pl.BlockSpec((None, tb, w), lambda bi, g, t: (bi, t, hq + 2 * ng_blocks + g)),
            pl.BlockSpec((None, tb, w), lambda bi, g, t: (bi, t, hz + g)),
            pl.BlockSpec((None, tb, LANES), lambda bi, g, t: (bi, t, 0)),
            pl.BlockSpec((GDN_CONV, w), lambda bi, g, t: (0, g)),
            pl.BlockSpec((GDN_CONV, w), lambda bi, g, t: (0, ng_blocks + g)),
            pl.BlockSpec((GDN_CONV, w), lambda bi, g, t: (0, 2 * ng_blocks + g)),
            pl.BlockSpec((1, LANES), lambda bi, g, t: (0, 0)),
            pl.BlockSpec((1, LANES), lambda bi, g, t: (0, 0)),
            pl.BlockSpec((1, LANES), lambda bi, g, t: (0, 0)),
        ],
        out_specs=pl.BlockSpec((None, tb, w), lambda bi, g, t: (bi, t, g)),
        out_shape=jax.ShapeDtypeStruct((b, s, nh * GDN_D), BF16),
        scratch_shapes=[
            pltpu.VMEM((hg, GDN_D, GDN_D), F32),
            pltpu.VMEM((3, SUBLANES, w), F32),
            pltpu.VMEM((SUBLANES + tb, w), F32),
        ],
        compiler_params=_cparams(("arbitrary", "arbitrary", "arbitrary")),
        name="gdn",
    )(proj, proj, proj, proj, small, conv_w, conv_w, conv_w, alog_v, dtb_v, ng)


DSA_QB = 128
DSA_KC = 512
DSA_ONES = 16
_KEY_NEG_INF = np.int32(np.array(0xFF800000, np.uint32).view(np.int32) ^ np.int32(0x7FFFFFFF))
_KEY_POS_INF_P1 = np.int32(0x7F800001)


def _key_to_f32(key):
    bits = key ^ ((key >> 31) & jnp.int32(0x7FFFFFFF))
    return lax.bitcast_convert_type(bits, F32)


def _tree_sum(xs):
    xs = list(xs)
    while len(xs) > 1:
        nxt = [xs[i] + xs[i + 1] for i in range(0, len(xs) - 1, 2)]
        if len(xs) % 2:
            nxt.append(xs[-1])
        xs = nxt
    return xs[0]


def _count_rows(hit):
    rows, n = hit.shape
    parts = hit.reshape(rows // SUBLANES, SUBLANES, n)
    return _tree_sum([parts[i] for i in range(rows // SUBLANES)])


def _dsa_body(q_ref, iq_ref, smq_ref, k_ref, v_ref, smk_ref, ikg_ref, o_ref,
              ikn_ref, vt_ref, s_ref, jsel_ref, m_ref, alpha_ref, acc_ref, lg_ref, *, seq, k_top):
    qb = pl.program_id(1)
    kc = DSA_KC
    nq = DSA_QB
    rep = DSA_HEADS // DSA_KV_HEADS

    @pl.when(qb == 0)
    def _():
        def prep(i, carry):
            rs = pl.ds(pl.multiple_of(i * kc, kc), kc)
            sm = smk_ref[rs, :]
            lane = lax.broadcasted_iota(I32, sm.shape, 1)
            isk = lane < IDX_DIM
            mean = jnp.sum(jnp.where(isk, sm, 0.0), axis=-1, keepdims=True) * (1.0 / IDX_DIM)
            xc = jnp.where(isk, sm - mean, 0.0)
            var = jnp.sum(xc * xc, axis=-1, keepdims=True) * (1.0 / IDX_DIM)
            y = xc * lax.rsqrt(var + EPS) * ikg_ref[...]
            ikn_ref[0, rs, :] = y.astype(BF16)
            ikn_ref[1, rs, :] = pltpu.roll(y, IDX_DIM, axis=1).astype(BF16)
            vv = v_ref[rs, :].astype(F32)
            for g in range(DSA_KV_HEADS):
                vt_ref[g, i, 0:DSA_DH, :] = vv[:, g * DSA_DH:(g + 1) * DSA_DH].T.astype(BF16)
                vt_ref[g, i, DSA_DH:DSA_DH + DSA_ONES, :] = jnp.ones((DSA_ONES, kc), BF16)
            return carry
        lax.fori_loop(0, seq // kc, prep, 0)

    q_lo = qb * nq
    n_kc = (q_lo + nq + kc - 1) // kc
    qpos = q_lo + lax.broadcasted_iota(I32, (1, nq), 1)

    iq = iq_ref[...]
    n_pair = IDX_HEADS // 2
    iq_rows = jnp.concatenate([iq[:, p * LANES:(p + 1) * LANES] for p in range(n_pair)], axis=0)
    iw_t = smq_ref[...].T * ((IDX_HEADS ** -0.5) * (IDX_DIM ** -0.5))

    def score_chunk(i, carry):
        rs = pl.ds(pl.multiple_of(i * kc, kc), kc)
        lo = _dot_nt(ikn_ref[0, rs, :], iq_rows)
        hi = _dot_nt(ikn_ref[1, rs, :], iq_rows)
        sc = jnp.zeros((kc, nq), F32)
        for p in range(n_pair):
            cs = slice(p * nq, (p + 1) * nq)
            sc = sc + iw_t[SM_IW + 2 * p:SM_IW + 2 * p + 1, :] * jnp.maximum(lo[:, cs], 0.0)
            sc = sc + iw_t[SM_IW + 2 * p + 1:SM_IW + 2 * p + 2, :] * jnp.maximum(hi[:, cs], 0.0)
        kpos = i * kc + lax.broadcasted_iota(I32, (kc, nq), 0)
        s_ref[rs, :] = jnp.where(kpos <= qpos, sc + 0.0, -jnp.inf)
        return carry
    lax.fori_loop(0, n_kc, score_chunk, 0)

    def count_ge(thr):
        def body(i, cnt):
            rs = pl.ds(pl.multiple_of(i * kc, kc), kc)
            return cnt + _count_rows(jnp.where(s_ref[rs, :] >= thr, 1.0, 0.0))
        cnt = lax.fori_loop(0, n_kc, body, jnp.zeros((SUBLANES, nq), F32))
        return jnp.sum(cnt, axis=0, keepdims=True)

    kf = jnp.float32(k_top)

    def midpoint(lo, hi):
        return (lo >> 1) + (hi >> 1) + (lo & hi & 1)

    def bisect(_, carry):
        lo, hi = carry
        mid = midpoint(lo, hi)
        ok = count_ge(_key_to_f32(mid)) >= kf
        return jnp.where(ok, mid, lo), jnp.where(ok, hi, mid)
    lo0 = jnp.full((1, nq), _KEY_NEG_INF, I32)
    hi0 = jnp.full((1, nq), _KEY_POS_INF_P1, I32)
    lo, _ = lax.fori_loop(0, 32, bisect, (lo0, hi0))
    thr = _key_to_f32(lo)

    def count_gt_ge(i, carry):
        c_gt, c_ge = carry
        rs = pl.ds(pl.multiple_of(i * kc, kc), kc)
        s = s_ref[rs, :]
        return (c_gt + _count_rows(jnp.where(s > thr, 1.0, 0.0)),
                c_ge + _count_rows(jnp.where(s >= thr, 1.0, 0.0)))
    z8 = jnp.zeros((SUBLANES, nq), F32)
    c_gt, c_ge = lax.fori_loop(0, n_kc, count_gt_ge, (z8, z8))
    n_gt = jnp.sum(c_gt, axis=0, keepdims=True)
    n_ge = jnp.sum(c_ge, axis=0, keepdims=True)
    need = kf - n_gt
    jsel_ref[...] = jnp.full((1, nq), seq, I32)

    @pl.when(jnp.max(n_ge) > kf)
    def _():
        def count_eq_upto(jmax):
            def body(i, cnt):
                rs = pl.ds(pl.multiple_of(i * kc, kc), kc)
                kpos = i * kc + lax.broadcasted_iota(I32, (kc, nq), 0)
                hit = jnp.where(s_ref[rs, :] == thr, jnp.where(kpos <= jmax, 1.0, 0.0), 0.0)
                return cnt + _count_rows(hit)
            cnt = lax.fori_loop(0, n_kc, body, jnp.zeros((SUBLANES, nq), F32))
            return jnp.sum(cnt, axis=0, keepdims=True)

        def jbisect(_, carry):
            lo_j, hi_j = carry
            mid = (lo_j + hi_j) >> 1
            ok = count_eq_upto(mid) >= need
            return jnp.where(ok, lo_j, mid), jnp.where(ok, mid, hi_j)
        n_iter = int(np.ceil(np.log2(seq))) + 1
        lo_j0 = jnp.full((1, nq), -1, I32)
        hi_j0 = jnp.full((1, nq), seq - 1, I32)
        _, hi_j = lax.fori_loop(0, n_iter, jbisect, (lo_j0, hi_j0))
        jsel_ref[...] = hi_j

    jsel = jsel_ref[...]

    def selection_bias(i):
        s = s_ref[pl.ds(pl.multiple_of(i * kc, kc), kc), :]
        kpos = i * kc + lax.broadcasted_iota(I32, (kc, nq), 0)
        tie = jnp.where(s == thr, jnp.where(kpos <= jsel, 0.0, NEG_BIG), NEG_BIG)
        sel = jnp.where(s > thr, 0.0, tie)
        return jnp.where(kpos <= qpos, sel, NEG_BIG)

    qall = (q_ref[...].astype(F32) * (DSA_DH ** -0.5 * np.log2(np.e))).astype(BF16)
    q_rows = [jnp.concatenate([qall[:, (g * rep + r) * DSA_DH:(g * rep + r + 1) * DSA_DH]
                               for r in range(rep)], axis=0) for g in range(DSA_KV_HEADS)]
    m_ref[...] = jnp.full(m_ref.shape, NEG_BIG, F32)
    acc_ref[...] = jnp.zeros(acc_ref.shape, F32)

    def logits_phase(i):
        rs = pl.ds(pl.multiple_of(i * kc, kc), kc)
        bias_w = jnp.concatenate([selection_bias(i)] * rep, axis=1)
        for g in range(DSA_KV_HEADS):
            logit = _dot_nt(k_ref[rs, g * DSA_DH:(g + 1) * DSA_DH], q_rows[g]) + bias_w
            lg_ref[g] = logit
            m_old = m_ref[g]
            m_new = jnp.maximum(m_old, jnp.max(logit, axis=0, keepdims=True))
            alpha_ref[g] = jnp.exp2(m_old - m_new)
            m_ref[g] = m_new

    def value_phase(i):
        for g in range(DSA_KV_HEADS):
            p = jnp.exp2(lg_ref[g] - m_ref[g]).astype(BF16)
            acc_ref[g] = acc_ref[g] * alpha_ref[g] + _dot(vt_ref[g, i], p)

    logits_phase(0)

    def attn_chunk(i, carry):
        value_phase(i - 1)
        logits_phase(i)
        return carry
    lax.fori_loop(1, n_kc, attn_chunk, 0)
    value_phase(n_kc - 1)

    for g in range(DSA_KV_HEADS):
        acc = acc_ref[g]
        ot = acc[0:DSA_DH] / acc[DSA_DH:DSA_DH + 1]
        for r in range(rep):
            hh = g * rep + r
            o_ref[:, hh * DSA_DH:(hh + 1) * DSA_DH] = ot[:, r * nq:(r + 1) * nq].T.astype(o_ref.dtype)


def _dsa(proj, small, ikg):
    b, s, _ = proj.shape
    nq = DSA_QB
    k_top = min(TOPK_MAX, s // 4)
    rep = DSA_HEADS // DSA_KV_HEADS
    return pl.pallas_call(
        functools.partial(_dsa_body, seq=s, k_top=k_top),
        grid=(b, s // nq),
        in_specs=[
            pl.BlockSpec((None, nq, DSA_HEADS * DSA_DH), lambda bi, qb: (bi, qb, COL_DQ // (DSA_HEADS * DSA_DH))),
            pl.BlockSpec((None, nq, IDX_HEADS * IDX_DIM), lambda bi, qb: (bi, qb, COL_IQ // (IDX_HEADS * IDX_DIM))),
            pl.BlockSpec((None, nq, LANES), lambda bi, qb: (bi, qb, 0)),
            pl.BlockSpec((None, s, DSA_KV_HEADS * DSA_DH), lambda bi, qb: (bi, 0, COL_DK // (DSA_KV_HEADS * DSA_DH))),
            pl.BlockSpec((None, s, DSA_KV_HEADS * DSA_DH), lambda bi, qb: (bi, 0, COL_DV // (DSA_KV_HEADS * DSA_DH))),
            pl.BlockSpec((None, s, LANES), lambda bi, qb: (bi, 0, 0)),
            pl.BlockSpec((1, LANES), lambda bi, qb: (0, 0)),
        ],
        out_specs=pl.BlockSpec((None, nq, DSA_HEADS * DSA_DH), lambda bi, qb: (bi, qb, 0)),
        out_shape=jax.ShapeDtypeStruct((b, s, DSA_HEADS * DSA_DH), BF16),
        scratch_shapes=[
            pltpu.VMEM((2, s, LANES), BF16),
            pltpu.VMEM((DSA_KV_HEADS, s // DSA_KC, DSA_DH + DSA_ONES, DSA_KC), BF16),
            pltpu.VMEM((s, nq), F32),
            pltpu.VMEM((1, nq), I32),
            pltpu.VMEM((DSA_KV_HEADS, 1, rep * nq), F32),
            pltpu.VMEM((DSA_KV_HEADS, 1, rep * nq), F32),
            pltpu.VMEM((DSA_KV_HEADS, DSA_DH + DSA_ONES, rep * nq), F32),
            pltpu.VMEM((DSA_KV_HEADS, DSA_KC, rep * nq), F32),
        ],
        compiler_params=_cparams(("arbitrary", "arbitrary")),
        name="dsa",
    )(proj, proj, small, proj, proj, small, ikg)


FFN_CHUNK = 256


def _mix_ffn_body(x_ref, ga_ref, gb_ref, ba_ref, bb_ref, og_ref, od_ref, wo_ref,
                  g2_ref, wup_ref, cw_ref, cb_ref, wdn_ref, gf_ref, o_ref,
                  carry_ref, xx_ref, *, tm):
    t = pl.program_id(1)

    @pl.when(t == 0)
    def _():
        carry_ref[...] = jnp.zeros_like(carry_ref)

    sa = _sigmoid(ga_ref[...].astype(F32) + ba_ref[...])
    sb = _sigmoid(gb_ref[...].astype(F32) + bb_ref[...])
    mixed = sa * og_ref[...].astype(F32) + sb * od_ref[...].astype(F32)
    h = x_ref[...] + _dot(mixed.astype(BF16), wo_ref[...])
    hn =((h * lax.rsqrt(jnp.mean(h * h, axis=-1, keepdims=True) + EPS)) * g2_ref[...]).astype(BF16)

    def up_cols(c0):
        cs = slice(c0, c0 + FFN_CHUNK)
        up = _dot(hn, wup_ref[:, cs])
        xx_ref[0:SUBLANES, cs] = carry_ref[:, cs]
        xx_ref[SUBLANES:SUBLANES + tm, cs] = up
        carry_ref[:, cs] = up[tm - SUBLANES:tm, :]

    def conv_cols(c0):
        cs = slice(c0, c0 + FFN_CHUNK)
        w = cw_ref[:, cs]
        y = cb_ref[:, cs] + w[0:1, :] * xx_ref[SUBLANES - (FFN_CONV - 1):SUBLANES - (FFN_CONV - 1) + tm, cs]
        for j in range(1, FFN_CONV):
            s0 = SUBLANES - (FFN_CONV - 1) + j
            y = y + w[j:j + 1, :] * xx_ref[s0:s0 + tm, cs]
        return y

    n_chunk = D_FF // FFN_CHUNK
    up_cols(0)
    up_cols(D_FF)
    acc = jnp.zeros((tm, D_MODEL), F32)
    for c in range(n_chunk):
        if c + 1 < n_chunk:
            up_cols((c + 1) * FFN_CHUNK)
            up_cols(D_FF + (c + 1) * FFN_CHUNK)
        gate = conv_cols(c * FFN_CHUNK)
        val = conv_cols(D_FF + c * FFN_CHUNK)
        act = (_silu(gate) * val).astype(BF16)
        acc = acc + _dot(act, wdn_ref[c * FFN_CHUNK:(c + 1) * FFN_CHUNK, :])
    h2 = h + acc
    o_ref[...] = (h2 * lax.rsqrt(jnp.mean(h2 * h2, axis=-1, keepdims=True) + EPS)) * gf_ref[...]


def _mix_ffn(x, proj, bias_a, bias_b, o_gdn, o_dsa, w_out, g2, w_up, conv_w, conv_b, w_down, gf, *, tm=256):
    b, s, _ = x.shape
    const = lambda bi, t: (0, 0)
    row = lambda bi, t: (bi, t, 0)
    return pl.pallas_call(
        functools.partial(_mix_ffn_body, tm=tm),
        grid=(b, s // tm),
        in_specs=[
            pl.BlockSpec((None, tm, D_MODEL), row),
            pl.BlockSpec((None, tm, D_MODEL), lambda bi, t: (bi, t, COL_GA // D_MODEL)),
            pl.BlockSpec((None, tm, D_MODEL), lambda bi, t: (bi, t, COL_GB // D_MODEL)),
            pl.BlockSpec((1, D_MODEL), const),
            pl.BlockSpec((1, D_MODEL), const),
            pl.BlockSpec((None, tm, D_MODEL), row),
            pl.BlockSpec((None, tm, D_MODEL), row),
            pl.BlockSpec((D_MODEL, D_MODEL), const, pipeline_mode=pl.Buffered(1)),
            pl.BlockSpec((1, D_MODEL), const),
            pl.BlockSpec((D_MODEL, 2 * D_FF), const, pipeline_mode=pl.Buffered(1)),
            pl.BlockSpec((FFN_CONV, 2 * D_FF), const),
            pl.BlockSpec((1, 2 * D_FF), const),
            pl.BlockSpec((D_FF, D_MODEL), const, pipeline_mode=pl.Buffered(1)),
            pl.BlockSpec((1, D_MODEL), const),
        ],
        out_specs=pl.BlockSpec((None, tm, D_MODEL), lambda bi, t: (bi, t, 0)),
        out_shape=jax.ShapeDtypeStruct((b, s, D_MODEL), F32),
        scratch_shapes=[
            pltpu.VMEM((SUBLANES, 2 * D_FF), F32),
            pltpu.VMEM((SUBLANES + tm, 2 * D_FF), F32),
        ],
        compiler_params=_cparams(("arbitrary", "arbitrary")),
        name="mix_ffn",
    )(x, proj, proj, bias_a, bias_b, o_gdn, o_dsa, w_out, g2, w_up, conv_w, conv_b, w_down, gf)


def _rearranged_w_in(w):
    o = np.cumsum([0, 3072, 1024, 8, 8, 1024, 256, 256, 512, 64, 8, 1024, 1024])
    g_qkv, g_z, g_a, g_b, d_q, d_k, d_v, i_q, i_k, i_w, gate_a, gate_b = (
        w[:, int(o[i]):int(o[i + 1])] for i in range(12))
    big = jnp.concatenate([g_qkv, g_z, d_q, d_k, d_v, i_q, gate_a, gate_b], axis=1).astype(BF16)
    pad = jnp.zeros((w.shape[0], LANES - (IDX_DIM + 3 * 8)), w.dtype)
    small = jnp.concatenate([i_k, g_a, g_b, i_w, pad], axis=1).astype(BF16)
    return big, small


def _lane_vec(v, offset):
    return jnp.zeros((1, LANES), F32).at[0, offset:offset + v.shape[0]].set(v.astype(F32))


def kernel(x, norm1_g, w_in, gdn_conv_w, gdn_a_log, gdn_dt_bias, gdn_norm_g, idx_k_norm_g,
           branch_gate_b, w_out, norm2_g, ffn_w_up, ffn_conv_w, ffn_conv_b, ffn_w_down, final_g):
    b, s, d = x.shape
    depth = norm1_g.shape[0]
    assert depth == 1, "the final RMSNorm is fused into the (single) channel-mixer call"
    h = x
    for l in range(depth):
        w_big, w_small = _rearranged_w_in(w_in[l])
        proj2, small2 = _inproj(h.reshape(b * s, d), norm1_g[l][None, :], w_big, w_small)
        proj = proj2.reshape(b, s, N_BIG)
        small = small2.reshape(b, s, LANES)
        o_gdn = _gdn(proj, small, gdn_conv_w[l], _lane_vec(gdn_a_log[l], SM_A),
                     _lane_vec(gdn_dt_bias[l], SM_A), gdn_norm_g[l][None, :].astype(F32))
        o_dsa = _dsa(proj, small, _lane_vec(idx_k_norm_g[l], SM_IK))
        gb = branch_gate_b[l]
        h = _mix_ffn(h, proj, gb[None, :D_MODEL], gb[None, D_MODEL:], o_gdn, o_dsa, w_out[l].astype(BF16),
                     norm2_g[l][None, :], ffn_w_up[l].astype(BF16), ffn_conv_w[l],
                     ffn_conv_b[l][None, :], ffn_w_down[l].astype(BF16), final_g[None, :])
    return h
```

```python
import functools

import jax
import jax.numpy as jnp
import numpy as np
from jax import lax
from jax.experimental import pallas as pl
from jax.experimental.pallas import tpu as pltpu

F32 = jnp.float32
BF16 = jnp.bfloat16
I32 = jnp.int32

LANES = 128
SUBLANES = 8
VMEM_LIMIT = 56 * 1024 * 1024

D_MODEL = 1024
GDN_HEADS = 8
GDN_D = 128
GDN_CONV = 4
DSA_HEADS = 8
DSA_KV_HEADS = 2
DSA_DH = 128
IDX_HEADS = 8
IDX_DIM = 64
TOPK_MAX = 256
D_FF = 2816
FFN_CONV = 3
EPS = 1e-6

COL_QKV = 0
COL_Z = 3072
COL_DQ = 4096
COL_DK = 5120
COL_DV = 5376
COL_IQ = 5632
COL_GA = 6144
COL_GB = 7168
N_BIG = 8192
SM_IK = 0
SM_A = 64
SM_B = 72
SM_IW = 80

NEG_BIG = -1e30


def _cparams(sem):
    return pltpu.CompilerParams(dimension_semantics=sem, vmem_limit_bytes=VMEM_LIMIT)


def _dot(a, b):
    return jnp.dot(a, b, preferred_element_type=F32)


def _dot_nt(a, b):
    return lax.dot_general(a, b, (((1,), (1,)), ((), ())), preferred_element_type=F32)


def _dot_tn(a, b):
    return lax.dot_general(a, b, (((0,), (0,)), ((), ())), preferred_element_type=F32)


def _split3(x):
    h = x.astype(BF16)
    r = x - h.astype(F32)
    m = r.astype(BF16)
    l = (r - m.astype(F32)).astype(BF16)
    return h, m, l


def _sigmoid(x):
    return 1.0 / (1.0 + jnp.exp(-x))


def _silu(x):
    h = 0.5 * x
    return h + h * jnp.tanh(h)


def _softplus(x):
    return jnp.maximum(x, 0.0) + jnp.log(1.0 + jnp.exp(-jnp.abs(x)))


def _inproj_body(x_ref, g_ref, w_ref, ws_ref, proj_ref, small_ref, *, col_chunk):
    x = x_ref[...]
    ms = jnp.mean(x * x, axis=-1, keepdims=True)
    u = ((x * lax.rsqrt(ms + EPS)) * g_ref[...]).astype(BF16)
    for c in range(N_BIG // col_chunk):
        sl = slice(c * col_chunk, (c + 1) * col_chunk)
        proj_ref[:, sl] = _dot(u, w_ref[:, sl]).astype(BF16)
    small_ref[...] = _dot(u, ws_ref[...])


def _inproj(x2, g, w_big, w_small, *, tm=256):
    n = x2.shape[0]
    const = lambda i: (0, 0)
    return pl.pallas_call(
        functools.partial(_inproj_body, col_chunk=1024),
        grid=(n // tm,),
        in_specs=[
            pl.BlockSpec((tm, D_MODEL), lambda i: (i, 0)),
            pl.BlockSpec((1, D_MODEL), const),
            pl.BlockSpec((D_MODEL, N_BIG), const, pipeline_mode=pl.Buffered(1)),
            pl.BlockSpec((D_MODEL, LANES), const, pipeline_mode=pl.Buffered(1)),
        ],
        out_specs=[
            pl.BlockSpec((tm, N_BIG), lambda i: (i, 0)),
            pl.BlockSpec((tm, LANES), lambda i: (i, 0)),
        ],
        out_shape=[
            jax.ShapeDtypeStruct((n, N_BIG), BF16),
            jax.ShapeDtypeStruct((n, LANES), F32),
        ],
        compiler_params=_cparams(("arbitrary",)),
        name="inproj",
    )(x2, g, w_big, w_small)


GDN_C = 128


def _tri_inverse(a_list):
    n = a_list[0].shape[0]
    row = lax.broadcasted_iota(I32, (n, n), 0)
    col = lax.broadcasted_iota(I32, (n, n), 1)
    eye = jnp.where(row == col, 1.0, 0.0).astype(F32)

    def same_block(size):
        sh = size.bit_length() - 1
        return (row >> sh) == (col >> sh)

    def square(ps):
        return [_dot(p, p).astype(BF16) for p in ps]

    def grow(ts, ps):
        return [t + _dot(t.astype(BF16), p) for t, p in zip(ts, ps)]

    p1 = [jnp.where(same_block(16), a, 0.0) for a in a_list]
    p2s = square([p.astype(BF16) for p in p1])
    ts = [eye - p for p in p1]
    p4s = square(p2s)
    ts = grow(ts, p2s)
    p8s = square(p4s)
    ts = grow(ts, p4s)
    ts = grow(ts, p8s)
    size = 16
    while size < n:
        in_pair = same_block(2 * size) & jnp.logical_not(same_block(size))
        offs = [jnp.where(in_pair, a, 0.0).astype(BF16) for a in a_list]
        tss = [t.astype(BF16) for t in ts]
        mids = [_dot(t2, o).astype(BF16) for t2, o in zip(tss, offs)]
        ts = [t - _dot(m, t2) for t, m, t2 in zip(ts, mids, tss)]
        size *= 2
    return ts


def _gdn_body(qp_ref, kp_ref, vp_ref, z_ref, sm_ref, wq_ref, wk_ref, wv_ref,
              alog_ref, dtb_ref, ng_ref, o_ref,
              state_ref, halo_ref, xx_ref, *, tb, hg):
    hgi = pl.program_id(1)
    t = pl.program_id(2)

    @pl.when(t == 0)
    def _():
        state_ref[...] = jnp.zeros_like(state_ref)
        halo_ref[...] = jnp.zeros_like(halo_ref)

    def conv_silu(idx, x_ref, w_ref):
        x = x_ref[...].astype(F32)
        xx_ref[0:SUBLANES, :] = halo_ref[idx]
        xx_ref[SUBLANES:SUBLANES + tb, :] = x
        halo_ref[idx] = x[tb - SUBLANES:tb, :]
        w = w_ref[...]
        y = w[GDN_CONV - 1:GDN_CONV, :] * x
        for j in range(GDN_CONV - 1):
            s0 = SUBLANES - (GDN_CONV - 1) + j
            y = y + w[j:j + 1, :] * xx_ref[s0:s0 + tb, :]
        return _silu(y)

    q_all = conv_silu(0, qp_ref, wq_ref)
    k_all = conv_silu(1, kp_ref, wk_ref)
    v_all = conv_silu(2, vp_ref, wv_ref)

    sm = sm_ref[...]
    ld_all = -jnp.exp(alog_ref[...]) * _softplus(sm + dtb_ref[...])
    beta_all = _sigmoid(sm)

    c = GDN_C
    row = lax.broadcasted_iota(I32, (c, c), 0)
    col = lax.broadcasted_iota(I32, (c, c), 1)
    lane = lax.broadcasted_iota(I32, (c, LANES), 1)
    tri_incl = jnp.where(row >= col, 1.0, 0.0).astype(BF16)
    ng = ng_ref[...]

    for ci in range(tb // c):
        rs = slice(ci * c, (ci + 1) * c)
        l1, l2, l3 = _split3(ld_all[rs])
        g_all = _dot(tri_incl, l1) + (_dot(tri_incl, l2) + _dot(tri_incl, l3))
        heads = range(hg)
        cols = [slice(j * GDN_D, (j + 1) * GDN_D) for j in heads]
        gc, bc, gamma, eg, g_last, qc, kc, kb, kc16 = ([None] * hg for _ in range(9))
        for j in heads:
            h = hgi * hg + j
            gcol = jnp.sum(jnp.where(lane == SM_A + h, g_all, 0.0), axis=-1, keepdims=True)
            bc[j] = jnp.sum(jnp.where(lane == SM_B + h, beta_all[rs], 0.0), axis=-1, keepdims=True)
            gc[j] = jnp.broadcast_to(gcol, (c, c))
            gamma[j] = jnp.exp(jnp.where(row >= col, gc[j] - gc[j].T, -jnp.inf))
            eg[j] = jnp.exp(gc[j])
            g_last[j] = gc[j][c - 1:c, :]
            q_h = q_all[rs, cols[j]]
            k_h = k_all[rs, cols[j]]
            qc[j] = q_h * (lax.rsqrt(jnp.sum(q_h * q_h, axis=-1, keepdims=True) + EPS) * (GDN_D ** -0.5))
            kc[j] = k_h * lax.rsqrt(jnp.sum(k_h * k_h, axis=-1, keepdims=True) + EPS)
            kb[j] = kc[j] * bc[j]
            kc16[j] = kc[j].astype(BF16)
        a = [jnp.where(row > col, _dot_nt(kb[j].astype(BF16), kc16[j]) * gamma[j], 0.0) for j in heads]
        att = [(_dot_nt(qc[j].astype(BF16), kc16[j]) * gamma[j]).astype(BF16) for j in heads]
        tinv = [t.astype(BF16) for t in _tri_inverse(a)]
        u = [_dot(tinv[j], (v_all[rs, cols[j]] * bc[j]).astype(BF16)) for j in heads]
        w = [_dot(tinv[j], (kb[j] * eg[j]).astype(BF16)).astype(BF16) for j in heads]
        qg = [(qc[j] * eg[j]).astype(BF16) for j in heads]
        kd = [(kc[j] * jnp.exp(g_last[j] - gc[j])).astype(BF16) for j in heads]

        s_old = [state_ref[j] for j in heads]
        s16 = [s.astype(BF16) for s in s_old]
        v_new16 = [(u[j] - _dot(w[j], s16[j])).astype(BF16) for j in heads]
        o = [_dot(qg[j], s16[j]) + _dot(att[j], v_new16[j]) for j in heads]
        for j in heads:
            state_ref[j] = s_old[j] * jnp.exp(g_last[j]) + _dot_tn(kd[j], v_new16[j])
        for j in heads:
            y = o[j] * lax.rsqrt(jnp.mean(o[j] * o[j], axis=-1, keepdims=True) + EPS) * ng
            o_ref[rs, cols[j]] = (y * _silu(z_ref[rs, cols[j]].astype(F32))).astype(o_ref.dtype)


def _gdn(proj, small, conv_w, alog_v, dtb_v, ng, *, tb=128, hg=8):
    b, s, _ = proj.shape
    nh = GDN_HEADS
    w = hg * GDN_D
    ng_blocks = nh // hg
    hq = COL_QKV // w
    hz = COL_Z // w
    return pl.pallas_call(
        functools.partial(_gdn_body, tb=tb, hg=hg),
        grid=(b, ng_blocks, s // tb),
        in_specs=[
            pl.BlockSpec((None, tb, w), lambda bi, g, t: (bi, t, hq + g)),
            pl.BlockSpec((None, tb, w), lambda bi, g, t: (bi, t, hq + ng_blocks + g)),
            pl.BlockSpec((None, tb, w), lambda bi, g, t: (bi, t, hq + 2 * ng_blocks + g)),
            pl.BlockSpec((None, tb, w), lambda bi, g, t: (bi, t, hz + g)),
            pl.BlockSpec((None, tb, LANES), lambda bi, g, t: (bi, t, 0)),
            pl.BlockSpec((GDN_CONV, w), lambda bi, g, t: (0, g)),
            pl.BlockSpec((GDN_CONV, w), lambda bi, g, t: (0, ng_blocks + g)),
            pl.BlockSpec((GDN_CONV, w), lambda bi, g, t: (0, 2 * ng_blocks + g)),
            pl.BlockSpec((1, LANES), lambda bi, g, t: (0, 0)),
            pl.BlockSpec((1, LANES), lambda bi, g, t: (0, 0)),
            pl.BlockSpec((1, LANES), lambda bi, g, t: (0, 0)),
        ],
        out_specs=pl.BlockSpec((None, tb, w), lambda bi, g, t: (bi, t, g)),
        out_shape=jax.ShapeDtypeStruct((b, s, nh * GDN_D), BF16),
        scratch_shapes=[
            pltpu.VMEM((hg, GDN_D, GDN_D), F32),
            pltpu.VMEM((3, SUBLANES, w), F32),
            pltpu.VMEM((SUBLANES + tb, w), F32),
        ],
        compiler_params=_cparams(("arbitrary", "arbitrary", "arbitrary")),
        name="gdn",
    )(proj, proj, proj, proj, small, conv_w, conv_w, conv_w, alog_v, dtb_v, ng)


DSA_QB = 128
DSA_KC = 512
DSA_ONES = 16
_KEY_NEG_INF = np.int32(np.array(0xFF800000, np.uint32).view(np.int32) ^ np.int32(0x7FFFFFFF))
_KEY_POS_INF_P1 = np.int32(0x7F800001)


def _key_to_f32(key):
    bits = key ^ ((key >> 31) & jnp.int32(0x7FFFFFFF))
    return lax.bitcast_convert_type(bits, F32)


def _tree_sum(xs):
    xs = list(xs)
    while len(xs) > 1:
        nxt = [xs[i] + xs[i + 1] for i in range(0, len(xs) - 1, 2)]
        if len(xs) % 2:
            nxt.append(xs[-1])
        xs = nxt
    return xs[0]


def _count_rows(hit):
    rows, n = hit.shape
    parts = hit.reshape(rows // SUBLANES, SUBLANES, n)
    return _tree_sum([parts[i] for i in range(rows // SUBLANES)])


def _dsa_body(q_ref, iq_ref, smq_ref, k_ref, v_ref, smk_ref, ikg_ref, o_ref,
              ikn_ref, vt_ref, s_ref, thrkey_ref, cnt_ref, jsel_ref, m_ref, alpha_ref, acc_ref, lg_ref, *, seq, k_top):
    qb = pl.program_id(1)
    kc = DSA_KC
    nq = DSA_QB
    rep = DSA_HEADS // DSA_KV_HEADS

    @pl.when(qb == 0)
    def _():
        def prep(i, carry):
            rs = pl.ds(pl.multiple_of(i * kc, kc), kc)
            sm = smk_ref[rs, :]
            lane = lax.broadcasted_iota(I32, sm.shape, 1)
            isk = lane < IDX_DIM
            mean = jnp.sum(jnp.where(isk, sm, 0.0), axis=-1, keepdims=True) * (1.0 / IDX_DIM)
            xc = jnp.where(isk, sm - mean, 0.0)
            var = jnp.sum(xc * xc, axis=-1, keepdims=True) * (1.0 / IDX_DIM)
            y = xc * lax.rsqrt(var + EPS) * ikg_ref[...]
            ikn_ref[0, rs, :] = y.astype(BF16)
            ikn_ref[1, rs, :] = pltpu.roll(y, IDX_DIM, axis=1).astype(BF16)
            vv = v_ref[rs, :].astype(F32)
            for g in range(DSA_KV_HEADS):
                vt_ref[g, i, 0:DSA_DH, :] = vv[:, g * DSA_DH:(g + 1) * DSA_DH].T.astype(BF16)
                vt_ref[g, i, DSA_DH:DSA_DH + DSA_ONES, :] = jnp.ones((DSA_ONES, kc), BF16)
            return carry
        lax.fori_loop(0, seq // kc, prep, 0)

    q_lo = qb * nq
    n_kc = (q_lo + nq + kc - 1) // kc
    qpos = q_lo + lax.broadcasted_iota(I32, (1, nq), 1)

    iq = iq_ref[...]
    n_pair = IDX_HEADS // 2
    iq_rows = jnp.concatenate([iq[:, p * LANES:(p + 1) * LANES] for p in range(n_pair)], axis=0)
    iw_t = smq_ref[...].T * ((IDX_HEADS ** -0.5) * (IDX_DIM ** -0.5))

    def score_chunk(i, carry):
        rs = pl.ds(pl.multiple_of(i * kc, kc), kc)
        lo = _dot_nt(ikn_ref[0, rs, :], iq_rows)
        hi = _dot_nt(ikn_ref[1, rs, :], iq_rows)
        sc = jnp.zeros((kc, nq), F32)
        for p in range(n_pair):
            cs = slice(p * nq, (p + 1) * nq)
            sc = sc + iw_t[SM_IW + 2 * p:SM_IW + 2 * p + 1, :] * jnp.maximum(lo[:, cs], 0.0)
            sc = sc + iw_t[SM_IW + 2 * p + 1:SM_IW + 2 * p + 2, :] * jnp.maximum(hi[:, cs], 0.0)
        kpos = i * kc + lax.broadcasted_iota(I32, (kc, nq), 0)
        s_ref[rs, :] = jnp.where(kpos <= qpos, sc + 0.0, -jnp.inf)
        return carry
    lax.fori_loop(0, n_kc, score_chunk, 0)

    kf = jnp.float32(k_top)

    def midpoint(lo, hi):
        return (lo >> 1) + (hi >> 1) + (lo & hi & 1)

    def bisect_static(n):
        def count_ge(thr):
            parts = [_count_rows(jnp.where(s_ref[c * kc:(c + 1) * kc, :] >= thr, 1.0, 0.0)) for c in range(n)]
            return jnp.sum(_tree_sum(parts), axis=0, keepdims=True)

        def bisect(_, carry):
            lo, hi = carry
            mid = midpoint(lo, hi)
            ok = count_ge(_key_to_f32(mid)) >= kf
            return jnp.where(ok, mid, lo), jnp.where(ok, hi, mid)
        lo0 = jnp.full((1, nq), _KEY_NEG_INF, I32)
        hi0 = jnp.full((1, nq), _KEY_POS_INF_P1, I32)
        lo, _ = lax.fori_loop(0, 32, bisect, (lo0, hi0))
        thrkey_ref[...] = lo
        thr = _key_to_f32(lo)
        gt = [_count_rows(jnp.where(s_ref[c * kc:(c + 1) * kc, :] > thr, 1.0, 0.0)) for c in range(n)]
        ge = [_count_rows(jnp.where(s_ref[c * kc:(c + 1) * kc, :] >= thr, 1.0, 0.0)) for c in range(n)]
        cnt_ref[0] = jnp.sum(_tree_sum(gt), axis=0, keepdims=True)
        cnt_ref[1] = jnp.sum(_tree_sum(ge), axis=0, keepdims=True)

    for n in range(1, seq // kc + 1):
        pl.when(n_kc == n)(functools.partial(bisect_static, n))
    thr = _key_to_f32(thrkey_ref[...])

    n_gt = cnt_ref[0]
    n_ge = cnt_ref[1]
    need = kf - n_gt
    jsel_ref[...] = jnp.full((1, nq), seq, I32)

    @pl.when(jnp.max(n_ge) > kf)
    def _():
        def count_eq_upto(jmax):
            def body(i, cnt):
                rs = pl.ds(pl.multiple_of(i * kc, kc), kc)
                kpos = i * kc + lax.broadcasted_iota(I32, (kc, nq), 0)
                hit = jnp.where(s_ref[rs, :] == thr, jnp.where(kpos <= jmax, 1.0, 0.0), 0.0)
                return cnt + _count_rows(hit)
            cnt = lax.fori_loop(0, n_kc, body, jnp.zeros((SUBLANES, nq), F32))
            return jnp.sum(cnt, axis=0, keepdims=True)

        def jbisect(_, carry):
            lo_j, hi_j = carry
            mid = (lo_j + hi_j) >> 1
            ok = count_eq_upto(mid) >= need
            return jnp.where(ok, lo_j, mid), jnp.where(ok, mid, hi_j)
        n_iter = int(np.ceil(np.log2(seq))) + 1
        lo_j0 = jnp.full((1, nq), -1, I32)
        hi_j0 = jnp.full((1, nq), seq - 1, I32)
        _, hi_j = lax.fori_loop(0, n_iter, jbisect, (lo_j0, hi_j0))
        jsel_ref[...] = hi_j

    jsel = jsel_ref[...]

    def selection_bias(i):
        s = s_ref[pl.ds(pl.multiple_of(i * kc, kc), kc), :]
        kpos = i * kc + lax.broadcasted_iota(I32, (kc, nq), 0)
        tie = jnp.where(s == thr, jnp.where(kpos <= jsel, 0.0, NEG_BIG), NEG_BIG)
        sel = jnp.where(s > thr, 0.0, tie)
        return jnp.where(kpos <= qpos, sel, NEG_BIG)

    qall = (q_ref[...].astype(F32) * (DSA_DH ** -0.5 * np.log2(np.e))).astype(BF16)
    q_rows = [jnp.concatenate([qall[:, (g * rep + r) * DSA_DH:(g * rep + r + 1) * DSA_DH]
                               for r in range(rep)], axis=0) for g in range(DSA_KV_HEADS)]
    m_ref[...] = jnp.full(m_ref.shape, NEG_BIG, F32)
    acc_ref[...] = jnp.zeros(acc_ref.shape, F32)

    def logits_phase(i):
        rs = pl.ds(pl.multiple_of(i * kc, kc), kc)
        bias_w = jnp.concatenate([selection_bias(i)] * rep, axis=1)
        for g in range(DSA_KV_HEADS):
            logit = _dot_nt(k_ref[rs, g * DSA_DH:(g + 1) * DSA_DH], q_rows[g]) + bias_w
            lg_ref[g] = logit
            m_old = m_ref[g]
            m_new = jnp.maximum(m_old, jnp.max(logit, axis=0, keepdims=True))
            alpha_ref[g] = jnp.exp2(m_old - m_new)
            m_ref[g] = m_new

    def value_phase(i):
        for g in range(DSA_KV_HEADS):
            p = jnp.exp2(lg_ref[g] - m_ref[g]).astype(BF16)
            acc_ref[g] = acc_ref[g] * alpha_ref[g] + _dot(vt_ref[g, i], p)

    logits_phase(0)

    def attn_chunk(i, carry):
        value_phase(i - 1)
        logits_phase(i)
        return carry
    lax.fori_loop(1, n_kc, attn_chunk, 0)
    value_phase(n_kc - 1)

    for g in range(DSA_KV_HEADS):
        acc = acc_ref[g]
        ot = acc[0:DSA_DH] / acc[DSA_DH:DSA_DH + 1]
        for r in range(rep):
            hh = g * rep + r
            o_ref[:, hh * DSA_DH:(hh + 1) * DSA_DH] = ot[:, r * nq:(r + 1) * nq].T.astype(o_ref.dtype)


def _dsa(proj, small, ikg):
    b, s, _ = proj.shape
    nq = DSA_QB
    k_top = min(TOPK_MAX, s // 4)
    rep = DSA_HEADS // DSA_KV_HEADS
    return pl.pallas_call(
        functools.partial(_dsa_body, seq=s, k_top=k_top),
        grid=(b, s // nq),
        in_specs=[
            pl.BlockSpec((None, nq, DSA_HEADS * DSA_DH), lambda bi, qb: (bi, qb, COL_DQ // (DSA_HEADS * DSA_DH))),
            pl.BlockSpec((None, nq, IDX_HEADS * IDX_DIM), lambda bi, qb: (bi, qb, COL_IQ // (IDX_HEADS * IDX_DIM))),
            pl.BlockSpec((None, nq, LANES), lambda bi, qb: (bi, qb, 0)),
            pl.BlockSpec((None, s, DSA_KV_HEADS * DSA_DH), lambda bi, qb: (bi, 0, COL_DK // (DSA_KV_HEADS * DSA_DH))),
            pl.BlockSpec((None, s, DSA_KV_HEADS * DSA_DH), lambda bi, qb: (bi, 0, COL_DV // (DSA_KV_HEADS * DSA_DH))),
            pl.BlockSpec((None, s, LANES), lambda bi, qb: (bi, 0, 0)),
            pl.BlockSpec((1, LANES), lambda bi, qb: (0, 0)),
        ],
        out_specs=pl.BlockSpec((None, nq, DSA_HEADS * DSA_DH), lambda bi, qb: (bi, qb, 0)),
        out_shape=jax.ShapeDtypeStruct((b, s, DSA_HEADS * DSA_DH), BF16),
        scratch_shapes=[
            pltpu.VMEM((2, s, LANES), BF16),
            pltpu.VMEM((DSA_KV_HEADS, s // DSA_KC, DSA_DH + DSA_ONES, DSA_KC), BF16),
            pltpu.VMEM((s, nq), F32),
            pltpu.VMEM((1, nq), I32),
            pltpu.VMEM((2, 1, nq), F32),
            pltpu.VMEM((1, nq), I32),
            pltpu.VMEM((DSA_KV_HEADS, 1, rep * nq), F32),
            pltpu.VMEM((DSA_KV_HEADS, 1, rep * nq), F32),
            pltpu.VMEM((DSA_KV_HEADS, DSA_DH + DSA_ONES, rep * nq), F32),
            pltpu.VMEM((DSA_KV_HEADS, DSA_KC, rep * nq), F32),
        ],
        compiler_params=_cparams(("arbitrary", "arbitrary")),
        name="dsa",
    )(proj, proj, small, proj, proj, small, ikg)


FFN_CHUNK = 256


def _mix_ffn_body(x_ref, ga_ref, gb_ref, ba_ref, bb_ref, og_ref, od_ref, wo_ref,
                  g2_ref, wup_ref, cw_ref, cb_ref, wdn_ref, gf_ref, o_ref,
                  carry_ref, xx_ref, *, tm):
    t = pl.program_id(1)

    @pl.when(t == 0)
    def _():
        carry_ref[...] = jnp.zeros_like(carry_ref)

    sa = _sigmoid(ga_ref[...].astype(F32) + ba_ref[...])
    sb = _sigmoid(gb_ref[...].astype(F32) + bb_ref[...])
    mixed = sa * og_ref[...].astype(F32) + sb * od_ref[...].astype(F32)
    h = x_ref[...] + _dot(mixed.astype(BF16), wo_ref[...])
    hn =((h * lax.rsqrt(jnp.mean(h * h, axis=-1, keepdims=True) + EPS)) * g2_ref[...]).astype(BF16)

    def up_cols(c0):
        cs = slice(c0, c0 + FFN_CHUNK)
        up = _dot(hn, wup_ref[:, cs])
        xx_ref[0:SUBLANES, cs] = carry_ref[:, cs]
        xx_ref[SUBLANES:SUBLANES + tm, cs] = up
        carry_ref[:, cs] = up[tm - SUBLANES:tm, :]

    def conv_cols(c0):
        cs = slice(c0, c0 + FFN_CHUNK)
        w = cw_ref[:, cs]
        y = cb_ref[:, cs] + w[0:1, :] * xx_ref[SUBLANES - (FFN_CONV - 1):SUBLANES - (FFN_CONV - 1) + tm, cs]
        for j in range(1, FFN_CONV):
            s0 = SUBLANES - (FFN_CONV - 1) + j
            y = y + w[j:j + 1, :] * xx_ref[s0:s0 + tm, cs]
        return y

    n_chunk = D_FF // FFN_CHUNK
    up_cols(0)
    up_cols(D_FF)
    acc = jnp.zeros((tm, D_MODEL), F32)
    for c in range(n_chunk):
        if c + 1 < n_chunk:
            up_cols((c + 1) * FFN_CHUNK)
            up_cols(D_FF + (c + 1) * FFN_CHUNK)
        gate = conv_cols(c * FFN_CHUNK)
        val = conv_cols(D_FF + c * FFN_CHUNK)
        act = (_silu(gate) * val).astype(BF16)
        acc = acc + _dot(act, wdn_ref[c * FFN_CHUNK:(c + 1) * FFN_CHUNK, :])
    h2 = h + acc
    o_ref[...] = (h2 * lax.rsqrt(jnp.mean(h2 * h2, axis=-1, keepdims=True) + EPS)) * gf_ref[...]


def _mix_ffn(x, proj, bias_a, bias_b, o_gdn, o_dsa, w_out, g2, w_up, conv_w, conv_b, w_down, gf, *, tm=256):
    b, s, _ = x.shape
    const = lambda bi, t: (0, 0)
    row = lambda bi, t: (bi, t, 0)
    return pl.pallas_call(
        functools.partial(_mix_ffn_body, tm=tm),
        grid=(b, s // tm),
        in_specs=[
            pl.BlockSpec((None, tm, D_MODEL), row),
            pl.BlockSpec((None, tm, D_MODEL), lambda bi, t: (bi, t, COL_GA // D_MODEL)),
            pl.BlockSpec((None, tm, D_MODEL), lambda bi, t: (bi, t, COL_GB // D_MODEL)),
            pl.BlockSpec((1, D_MODEL), const),
            pl.BlockSpec((1, D_MODEL), const),
            pl.BlockSpec((None, tm, D_MODEL), row),
            pl.BlockSpec((None, tm, D_MODEL), row),
            pl.BlockSpec((D_MODEL, D_MODEL), const, pipeline_mode=pl.Buffered(1)),
            pl.BlockSpec((1, D_MODEL), const),
            pl.BlockSpec((D_MODEL, 2 * D_FF), const, pipeline_mode=pl.Buffered(1)),
            pl.BlockSpec((FFN_CONV, 2 * D_FF), const),
            pl.BlockSpec((1, 2 * D_FF), const),
            pl.BlockSpec((D_FF, D_MODEL), const, pipeline_mode=pl.Buffered(1)),
            pl.BlockSpec((1, D_MODEL), const),
        ],
        out_specs=pl.BlockSpec((None, tm, D_MODEL), lambda bi, t: (bi, t, 0)),
        out_shape=jax.ShapeDtypeStruct((b, s, D_MODEL), F32),
        scratch_shapes=[
            pltpu.VMEM((SUBLANES, 2 * D_FF), F32),
            pltpu.VMEM((SUBLANES + tm, 2 * D_FF), F32),
        ],
        compiler_params=_cparams(("arbitrary", "arbitrary")),
        name="mix_ffn",
    )(x, proj, proj, bias_a, bias_b, o_gdn, o_dsa, w_out, g2, w_up, conv_w, conv_b, w_down, gf)


def _rearranged_w_in(w):
    o = np.cumsum([0, 3072, 1024, 8, 8, 1024, 256, 256, 512, 64, 8, 1024, 1024])
    g_qkv, g_z, g_a, g_b, d_q, d_k, d_v, i_q, i_k, i_w, gate_a, gate_b = (
        w[:, int(o[i]):int(o[i + 1])] for i in range(12))
    big = jnp.concatenate([g_qkv, g_z, d_q, d_k, d_v, i_q, gate_a, gate_b], axis=1).astype(BF16)
    pad = jnp.zeros((w.shape[0], LANES - (IDX_DIM + 3 * 8)), w.dtype)
    small = jnp.concatenate([i_k, g_a, g_b, i_w, pad], axis=1).astype(BF16)
    return big, small


def _lane_vec(v, offset):
    return jnp.zeros((1, LANES), F32).at[0, offset:offset + v.shape[0]].set(v.astype(F32))


def kernel(x, norm1_g, w_in, gdn_conv_w, gdn_a_log, gdn_dt_bias, gdn_norm_g, idx_k_norm_g,
           branch_gate_b, w_out, norm2_g, ffn_w_up, ffn_conv_w, ffn_conv_b, ffn_w_down, final_g):
    b, s, d = x.shape
    depth = norm1_g.shape[0]
    assert depth == 1, "the final RMSNorm is fused into the (single) channel-mixer call"
    h = x
    for l in range(depth):
        w_big, w_small = _rearranged_w_in(w_in[l])
        proj2, small2 = _inproj(h.reshape(b * s, d), norm1_g[l][None, :], w_big, w_small)
        proj = proj2.reshape(b, s, N_BIG)
        small = small2.reshape(b, s, LANES)
        o_gdn = _gdn(proj, small, gdn_conv_w[l], _lane_vec(gdn_a_log[l], SM_A),
                     _lane_vec(gdn_dt_bias[l], SM_A), gdn_norm_g[l][None, :].astype(F32))
        o_dsa = _dsa(proj, small, _lane_vec(idx_k_norm_g[l], SM_IK))
        gb = branch_gate_b[l]
        h = _mix_ffn(h, proj, gb[None, :D_MODEL], gb[None, D_MODEL:], o_gdn, o_dsa, w_out[l].astype(BF16),
                     norm2_g[l][None, :], ffn_w_up[l].astype(BF16), ffn_conv_w[l],
                     ffn_conv_b[l][None, :], ffn_w_down[l].astype(BF16), final_g[None, :])
    return h
```

```python
import functools

import jax
import jax.numpy as jnp
import numpy as np
from jax import lax
from jax.experimental import pallas as pl
from jax.experimental.pallas import tpu as pltpu

F32 = jnp.float32
BF16 = jnp.bfloat16
I32 = jnp.int32

LANES = 128
SUBLANES = 8
VMEM_LIMIT = 56 * 1024 * 1024

D_MODEL = 1024
GDN_HEADS = 8
GDN_D = 128
GDN_CONV = 4
DSA_HEADS = 8
DSA_KV_HEADS = 2
DSA_DH = 128
IDX_HEADS = 8
IDX_DIM = 64
TOPK_MAX = 256
D_FF = 2816
FFN_CONV = 3
EPS = 1e-6

COL_QKV = 0
COL_Z = 3072
COL_DQ = 4096
COL_DK = 5120
COL_DV = 5376
COL_IQ = 5632
COL_GA = 6144
COL_GB = 7168
N_BIG = 8192
SM_IK = 0
SM_A = 64
SM_B = 72
SM_IW = 80

NEG_BIG = -1e30


def _cparams(sem):
    return pltpu.CompilerParams(dimension_semantics=sem, vmem_limit_bytes=VMEM_LIMIT)


def _dot(a, b):
    return jnp.dot(a, b, preferred_element_type=F32)


def _dot_nt(a, b):
    return lax.dot_general(a, b, (((1,), (1,)), ((), ())), preferred_element_type=F32)


def _dot_tn(a, b):
    return lax.dot_general(a, b, (((0,), (0,)), ((), ())), preferred_element_type=F32)


def _split3(x):
    h = x.astype(BF16)
    r = x - h.astype(F32)
    m = r.astype(BF16)
    l = (r - m.astype(F32)).astype(BF16)
    return h, m, l


def _sigmoid(x):
    return 1.0 / (1.0 + jnp.exp(-x))


def _silu(x):
    h = 0.5 * x
    return h + h * jnp.tanh(h)


def _softplus(x):
    return jnp.maximum(x, 0.0) + jnp.log(1.0 + jnp.exp(-jnp.abs(x)))


def _inproj_body(x_ref, g_ref, w_ref, ws_ref, proj_ref, small_ref, *, col_chunk):
    x = x_ref[...]
    ms = jnp.mean(x * x, axis=-1, keepdims=True)
    u = ((x * lax.rsqrt(ms + EPS)) * g_ref[...]).astype(BF16)
    for c in range(N_BIG // col_chunk):
        sl = slice(c * col_chunk, (c + 1) * col_chunk)
        proj_ref[:, sl] = _dot(u, w_ref[:, sl]).astype(BF16)
    small_ref[...] = _dot(u, ws_ref[...])


def _inproj(x2, g, w_big, w_small, *, tm=256):
    n = x2.shape[0]
    const = lambda i: (0, 0)
    return pl.pallas_call(
        functools.partial(_inproj_body, col_chunk=1024),
        grid=(n // tm,),
        in_specs=[
            pl.BlockSpec((tm, D_MODEL), lambda i: (i, 0)),
            pl.BlockSpec((1, D_MODEL), const),
            pl.BlockSpec((D_MODEL, N_BIG), const, pipeline_mode=pl.Buffered(1)),
            pl.BlockSpec((D_MODEL, LANES), const, pipeline_mode=pl.Buffered(1)),
        ],
        out_specs=[
            pl.BlockSpec((tm, N_BIG), lambda i: (i, 0)),
            pl.BlockSpec((tm, LANES), lambda i: (i, 0)),
        ],
        out_shape=[
            jax.ShapeDtypeStruct((n, N_BIG), BF16),
            jax.ShapeDtypeStruct((n, LANES), F32),
        ],
        compiler_params=_cparams(("arbitrary",)),
        name="inproj",
    )(x2, g, w_big, w_small)


GDN_C = 128


def _tri_inverse(a_list):
    n = a_list[0].shape[0]
    row = lax.broadcasted_iota(I32, (n, n), 0)
    col = lax.broadcasted_iota(I32, (n, n), 1)
    eye = jnp.where(row == col, 1.0, 0.0).astype(F32)

    def same_block(size):
        sh = size.bit_length() - 1
        return (row >> sh) == (col >> sh)

    def square(ps):
        return [_dot(p, p).astype(BF16) for p in ps]

    def grow(ts, ps):
        return [t + _dot(t.astype(BF16), p) for t, p in zip(ts, ps)]

    p1 = [jnp.where(same_block(16), a, 0.0) for a in a_list]
    p2s = square([p.astype(BF16) for p in p1])
    ts = [eye - p for p in p1]
    p4s = square(p2s)
    ts = grow(ts, p2s)
    p8s = square(p4s)
    ts = grow(ts, p4s)
    ts = grow(ts, p8s)
    size = 16
    while size < n:
        in_pair = same_block(2 * size) & jnp.logical_not(same_block(size))
        offs = [jnp.where(in_pair, a, 0.0).astype(BF16) for a in a_list]
        tss = [t.astype(BF16) for t in ts]
        mids = [_dot(t2, o).astype(BF16) for t2, o in zip(tss, offs)]
        ts = [t - _dot(m, t2) for t, m, t2 in zip(ts, mids, tss)]
        size *= 2
    return ts


def _gdn_body(qp_ref, kp_ref, vp_ref, z_ref, sm_ref, wq_ref, wk_ref, wv_ref,
              alog_ref, dtb_ref, ng_ref, o_ref,
              state_ref, halo_ref, xx_ref, *, tb, hg):
    hgi = pl.program_id(1)
    t = pl.program_id(2)

    @pl.when(t == 0)
    def _():
        state_ref[...] = jnp.zeros_like(state_ref)
        halo_ref[...] = jnp.zeros_like(halo_ref)

    def conv_silu(idx, x_ref, w_ref):
        x = x_ref[...].astype(F32)
        xx_ref[0:SUBLANES, :] = halo_ref[idx]
        xx_ref[SUBLANES:SUBLANES + tb, :] = x
        halo_ref[idx] = x[tb - SUBLANES:tb, :]
        w = w_ref[...]
        y = w[GDN_CONV - 1:GDN_CONV, :] * x
        for j in range(GDN_CONV - 1):
            s0 = SUBLANES - (GDN_CONV - 1) + j
            y = y + w[j:j + 1, :] * xx_ref[s0:s0 + tb, :]
        return _silu(y)

    q_all = conv_silu(0, qp_ref, wq_ref)
    k_all = conv_silu(1, kp_ref, wk_ref)
    v_all = conv_silu(2, vp_ref, wv_ref)

    sm = sm_ref[...]
    ld_all = -jnp.exp(alog_ref[...]) * _softplus(sm + dtb_ref[...])
    beta_all = _sigmoid(sm)

    c = GDN_C
    row = lax.broadcasted_iota(I32, (c, c), 0)
    col = lax.broadcasted_iota(I32, (c, c), 1)
    lane = lax.broadcasted_iota(I32, (c, LANES), 1)
    tri_incl = jnp.where(row >= col, 1.0, 0.0).astype(BF16)
    ng = ng_ref[...]

    for ci in range(tb // c):
        rs = slice(ci * c, (ci + 1) * c)
        l1, l2, l3 = _split3(ld_all[rs])
        g_all = _dot(tri_incl, l1) + (_dot(tri_incl, l2) + _dot(tri_incl, l3))
        heads = range(hg)
        cols = [slice(j * GDN_D, (j + 1) * GDN_D) for j in heads]
        gc, bc, gamma, eg, g_last, qc, kc, kb, kc16 = ([None] * hg for _ in range(9))
        for j in heads:
            h = hgi * hg + j
            gcol = jnp.sum(jnp.where(lane == SM_A + h, g_all, 0.0), axis=-1, keepdims=True)
            bc[j] = jnp.sum(jnp.where(lane == SM_B + h, beta_all[rs], 0.0), axis=-1, keepdims=True)
            gc[j] = jnp.broadcast_to(gcol, (c, c))
            gamma[j] = jnp.exp(jnp.where(row >= col, gc[j] - gc[j].T, -jnp.inf))
            eg[j] = jnp.exp(gc[j])
            g_last[j] = gc[j][c - 1:c, :]
            q_h = q_all[rs, cols[j]]
            k_h = k_all[rs, cols[j]]
            qc[j] = q_h * (lax.rsqrt(jnp.sum(q_h * q_h, axis=-1, keepdims=True) + EPS) * (GDN_D ** -0.5))
            kc[j] = k_h * lax.rsqrt(jnp.sum(k_h * k_h, axis=-1, keepdims=True) + EPS)
            kb[j] = kc[j] * bc[j]
            kc16[j] = kc[j].astype(BF16)
        a = [jnp.where(row > col, _dot_nt(kb[j].astype(BF16), kc16[j]) * gamma[j], 0.0) for j in heads]
        att = [(_dot_nt(qc[j].astype(BF16), kc16[j]) * gamma[j]).astype(BF16) for j in heads]
        tinv = [t.astype(BF16) for t in _tri_inverse(a)]
        u = [_dot(tinv[j], (v_all[rs, cols[j]] * bc[j]).astype(BF16)) for j in heads]
        w = [_dot(tinv[j], (kb[j] * eg[j]).astype(BF16)).astype(BF16) for j in heads]
        qg = [(qc[j] * eg[j]).astype(BF16) for j in heads]
        kd = [(kc[j] * jnp.exp(g_last[j] - gc[j])).astype(BF16) for j in heads]

        s_old = [state_ref[j] for j in heads]
        s16 = [s.astype(BF16) for s in s_old]
        v_new16 = [(u[j] - _dot(w[j], s16[j])).astype(BF16) for j in heads]
        o = [_dot(qg[j], s16[j]) + _dot(att[j], v_new16[j]) for j in heads]
        for j in heads:
            state_ref[j] = s_old[j] * jnp.exp(g_last[j]) + _dot_tn(kd[j], v_new16[j])
        for j in heads:
            y = o[j] * lax.rsqrt(jnp.mean(o[j] * o[j], axis=-1, keepdims=True) + EPS) * ng
            o_ref[rs, cols[j]] = (y * _silu(z_ref[rs, cols[j]].astype(F32))).astype(o_ref.dtype)


def _gdn(proj, small, conv_w, alog_v, dtb_v, ng, *, tb=128, hg=8):
    b, s, _ = proj.shape
    nh = GDN_HEADS
    w = hg * GDN_D
    ng_blocks = nh // hg
    hq = COL_QKV // w
    hz = COL_Z // w
    return pl.pallas_call(
        functools.partial(_gdn_body, tb=tb, hg=hg),
        grid=(b, ng_blocks, s // tb),
        in_specs=[
            pl.BlockSpec((None, tb, w), lambda bi, g, t: (bi, t, hq + g)),
            pl.BlockSpec((None, tb, w), lambda bi, g, t: (bi, t, hq + ng_blocks + g)),
            pl.BlockSpec((None, tb, w), lambda bi, g, t: (bi, t, hq + 2 * ng_blocks + g)),
            pl.BlockSpec((None, tb, w), lambda bi, g, t: (bi, t, hz + g)),
            pl.BlockSpec((None, tb, LANES), lambda bi, g, t: (bi, t, 0)),
            pl.BlockSpec((GDN_CONV, w), lambda bi, g, t: (0, g)),
            pl.BlockSpec((GDN_CONV, w), lambda bi, g, t: (0, ng_blocks + g)),
            pl.BlockSpec((GDN_CONV, w), lambda bi, g, t: (0, 2 * ng_blocks + g)),
            pl.BlockSpec((1, LANES), lambda bi, g, t: (0, 0)),
            pl.BlockSpec((1, LANES), lambda bi, g, t: (0, 0)),
            pl.BlockSpec((1, LANES), lambda bi, g, t: (0, 0)),
        ],
        out_specs=pl.BlockSpec((None, tb, w), lambda bi, g, t: (bi, t, g)),
        out_shape=jax.ShapeDtypeStruct((b, s, nh * GDN_D), BF16),
        scratch_shapes=[
            pltpu.VMEM((hg, GDN_D, GDN_D), F32),
            pltpu.VMEM((3, SUBLANES, w), F32),
            pltpu.VMEM((SUBLANES + tb, w), F32),
        ],
        compiler_params=_cparams(("arbitrary", "arbitrary", "arbitrary")),
        name="gdn",
    )(proj, proj, proj, proj, small, conv_w, conv_w, conv_w, alog_v, dtb_v, ng)


DSA_QB = 128
DSA_KC = 512
DSA_ONES = 16
_KEY_NEG_INF = np.int32(np.array(0xFF800000, np.uint32).view(np.int32) ^ np.int32(0x7FFFFFFF))
_KEY_POS_INF_P1 = np.int32(0x7F800001)


def _key_to_f32(key):
    bits = key ^ ((key >> 31) & jnp.int32(0x7FFFFFFF))
    return lax.bitcast_convert_type(bits, F32)


def _tree_sum(xs):
    xs = list(xs)
    while len(xs) > 1:
        nxt = [xs[i] + xs[i + 1] for i in range(0, len(xs) - 1, 2)]
        if len(xs) % 2:
            nxt.append(xs[-1])
        xs = nxt
    return xs[0]


def _count_rows(hit):
    rows, n = hit.shape
    parts = hit.reshape(rows // SUBLANES, SUBLANES, n)
    return _tree_sum([parts[i] for i in range(rows // SUBLANES)])


def _dsa_body(q_ref, iq_ref, smq_ref, k_ref, v_ref, smk_ref, ikg_ref, o_ref,
              ikn_ref, vt_ref, s_ref, thrkey_ref, cnt_ref, jsel_ref, m_ref, alpha_ref, acc_ref, lg_ref, *, seq, k_top):
    qb = pl.program_id(1)
    kc = DSA_KC
    nq = DSA_QB
    rep = DSA_HEADS // DSA_KV_HEADS

    @pl.when(qb == 0)
    def _():
        def prep(i, carry):
            rs = pl.ds(pl.multiple_of(i * kc, kc), kc)
            sm = smk_ref[rs, :]
            lane = lax.broadcasted_iota(I32, sm.shape, 1)
            isk = lane < IDX_DIM
            mean = jnp.sum(jnp.where(isk, sm, 0.0), axis=-1, keepdims=True) * (1.0 / IDX_DIM)
            xc = jnp.where(isk, sm - mean, 0.0)
            var = jnp.sum(xc * xc, axis=-1, keepdims=True) * (1.0 / IDX_DIM)
            y = xc * lax.rsqrt(var + EPS) * ikg_ref[...]
            ikn_ref[0, rs, :] = y.astype(BF16)
            ikn_ref[1, rs, :] = pltpu.roll(y, IDX_DIM, axis=1).astype(BF16)
            vv = v_ref[rs, :].astype(F32)
            for g in range(DSA_KV_HEADS):
                vt_ref[g, i, 0:DSA_DH, :] = vv[:, g * DSA_DH:(g + 1) * DSA_DH].T.astype(BF16)
                vt_ref[g, i, DSA_DH:DSA_DH + DSA_ONES, :] = jnp.ones((DSA_ONES, kc), BF16)
            return carry
        lax.fori_loop(0, seq // kc, prep, 0)

    q_lo = qb * nq
    n_kc = (q_lo + nq + kc - 1) // kc
    qpos = q_lo + lax.broadcasted_iota(I32, (1, nq), 1)

    iq = iq_ref[...]
    n_pair = IDX_HEADS // 2
    iq_t = jnp.concatenate([iq[:, p * LANES:(p + 1) * LANES].astype(F32).T for p in range(n_pair)],
                           axis=1).astype(BF16)
    iw_t = smq_ref[...].T * ((IDX_HEADS ** -0.5) * (IDX_DIM ** -0.5))

    def score_chunk(i, carry):
        rs = pl.ds(pl.multiple_of(i * kc, kc), kc)
        lo = _dot(ikn_ref[0, rs, :], iq_t)
        hi = _dot(ikn_ref[1, rs, :], iq_t)
        sc = jnp.zeros((kc, nq), F32)
        for p in range(n_pair):
            cs = slice(p * nq, (p + 1) * nq)
            sc = sc + iw_t[SM_IW + 2 * p:SM_IW + 2 * p + 1, :] * jnp.maximum(lo[:, cs], 0.0)
            sc = sc + iw_t[SM_IW + 2 * p + 1:SM_IW + 2 * p + 2, :] * jnp.maximum(hi[:, cs], 0.0)
        kpos = i * kc + lax.broadcasted_iota(I32, (kc, nq), 0)
        s_ref[rs, :] = jnp.where(kpos <= qpos, sc + 0.0, -jnp.inf)
        return carry
    lax.fori_loop(0, n_kc, score_chunk, 0)

    kf = jnp.float32(k_top)

    def midpoint(lo, hi):
        return (lo >> 1) + (hi >> 1) + (lo & hi & 1)

    def bisect_static(n):
        def count_ge(thr):
            parts = [_count_rows(jnp.where(s_ref[c * kc:(c + 1) * kc, :] >= thr, 1.0, 0.0)) for c in range(n)]
            return jnp.sum(_tree_sum(parts), axis=0, keepdims=True)

        def bisect(_, carry):
            lo, hi = carry
            mid = midpoint(lo, hi)
            ok = count_ge(_key_to_f32(mid)) >= kf
            return jnp.where(ok, mid, lo), jnp.where(ok, hi, mid)
        lo0 = jnp.full((1, nq), _KEY_NEG_INF, I32)
        hi0 = jnp.full((1, nq), _KEY_POS_INF_P1, I32)
        lo, _ = lax.fori_loop(0, 32, bisect, (lo0, hi0))
        thrkey_ref[...] = lo
        thr = _key_to_f32(lo)
        gt = [_count_rows(jnp.where(s_ref[c * kc:(c + 1) * kc, :] > thr, 1.0, 0.0)) for c in range(n)]
        ge = [_count_rows(jnp.where(s_ref[c * kc:(c + 1) * kc, :] >= thr, 1.0, 0.0)) for c in range(n)]
        cnt_ref[0] = jnp.sum(_tree_sum(gt), axis=0, keepdims=True)
        cnt_ref[1] = jnp.sum(_tree_sum(ge), axis=0, keepdims=True)

    for n in range(1, seq // kc + 1):
        pl.when(n_kc == n)(functools.partial(bisect_static, n))
    thr = _key_to_f32(thrkey_ref[...])

    n_gt = cnt_ref[0]
    n_ge = cnt_ref[1]
    need = kf - n_gt
    jsel_ref[...] = jnp.full((1, nq), seq, I32)

    @pl.when(jnp.max(n_ge) > kf)
    def _():
        def count_eq_upto(jmax):
            def body(i, cnt):
                rs = pl.ds(pl.multiple_of(i * kc, kc), kc)
                kpos = i * kc + lax.broadcasted_iota(I32, (kc, nq), 0)
                hit = jnp.where(s_ref[rs, :] == thr, jnp.where(kpos <= jmax, 1.0, 0.0), 0.0)
                return cnt + _count_rows(hit)
            cnt = lax.fori_loop(0, n_kc, body, jnp.zeros((SUBLANES, nq), F32))
            return jnp.sum(cnt, axis=0, keepdims=True)

        def jbisect(_, carry):
            lo_j, hi_j = carry
            mid = (lo_j + hi_j) >> 1
            ok = count_eq_upto(mid) >= need
            return jnp.where(ok, lo_j, mid), jnp.where(ok, mid, hi_j)
        n_iter = int(np.ceil(np.log2(seq))) + 1
        lo_j0 = jnp.full((1, nq), -1, I32)
        hi_j0 = jnp.full((1, nq), seq - 1, I32)
        _, hi_j = lax.fori_loop(0, n_iter, jbisect, (lo_j0, hi_j0))
        jsel_ref[...] = hi_j

    jsel = jsel_ref[...]

    def selection_bias(i):
        s = s_ref[pl.ds(pl.multiple_of(i * kc, kc), kc), :]
        kpos = i * kc + lax.broadcasted_iota(I32, (kc, nq), 0)
        tie = jnp.where(s == thr, jnp.where(kpos <= jsel, 0.0, NEG_BIG), NEG_BIG)
        sel = jnp.where(s > thr, 0.0, tie)
        return jnp.where(kpos <= qpos, sel, NEG_BIG)

    qall = q_ref[...].astype(F32) * (DSA_DH ** -0.5 * np.log2(np.e))
    q_t = [jnp.concatenate([qall[:, (g * rep + r) * DSA_DH:(g * rep + r + 1) * DSA_DH].T
                            for r in range(rep)], axis=1).astype(BF16) for g in range(DSA_KV_HEADS)]
    m_ref[...] = jnp.full(m_ref.shape, NEG_BIG, F32)
    acc_ref[...] = jnp.zeros(acc_ref.shape, F32)

    def logits_phase(i):
        rs = pl.ds(pl.multiple_of(i * kc, kc), kc)
        bias_w = jnp.concatenate([selection_bias(i)] * rep, axis=1)
        for g in range(DSA_KV_HEADS):
            logit = _dot(k_ref[rs, g * DSA_DH:(g + 1) * DSA_DH], q_t[g]) + bias_w
            lg_ref[g] = logit
            m_old = m_ref[g]
            m_new = jnp.maximum(m_old, jnp.max(logit, axis=0, keepdims=True))
            alpha_ref[g] = jnp.exp2(m_old - m_new)
            m_ref[g] = m_new

    def value_phase(i):
        for g in range(DSA_KV_HEADS):
            p = jnp.exp2(lg_ref[g] - m_ref[g]).astype(BF16)
            acc_ref[g] = acc_ref[g] * alpha_ref[g] + _dot(vt_ref[g, i], p)

    logits_phase(0)

    def attn_chunk(i, carry):
        value_phase(i - 1)
        logits_phase(i)
        return carry
    lax.fori_loop(1, n_kc, attn_chunk, 0)
    value_phase(n_kc - 1)

    for g in range(DSA_KV_HEADS):
        acc = acc_ref[g]
        ot = acc[0:DSA_DH] / acc[DSA_DH:DSA_DH + 1]
        for r in range(rep):
            hh = g * rep + r
            o_ref[:, hh * DSA_DH:(hh + 1) * DSA_DH] = ot[:, r * nq:(r + 1) * nq].T.astype(o_ref.dtype)


def _dsa(proj, small, ikg):
    b, s, _ = proj.shape
    nq = DSA_QB
    k_top = min(TOPK_MAX, s // 4)
    rep = DSA_HEADS // DSA_KV_HEADS
    return pl.pallas_call(
        functools.partial(_dsa_body, seq=s, k_top=k_top),
        grid=(b, s // nq),
        in_specs=[
            pl.BlockSpec((None, nq, DSA_HEADS * DSA_DH), lambda bi, qb: (bi, qb, COL_DQ // (DSA_HEADS * DSA_DH))),
            pl.BlockSpec((None, nq, IDX_HEADS * IDX_DIM), lambda bi, qb: (bi, qb, COL_IQ // (IDX_HEADS * IDX_DIM))),
            pl.BlockSpec((None, nq, LANES), lambda bi, qb: (bi, qb, 0)),
            pl.BlockSpec((None, s, DSA_KV_HEADS * DSA_DH), lambda bi, qb: (bi, 0, COL_DK // (DSA_KV_HEADS * DSA_DH))),
            pl.BlockSpec((None, s, DSA_KV_HEADS * DSA_DH), lambda bi, qb: (bi, 0, COL_DV // (DSA_KV_HEADS * DSA_DH))),
            pl.BlockSpec((None, s, LANES), lambda bi, qb: (bi, 0, 0)),
            pl.BlockSpec((1, LANES), lambda bi, qb: (0, 0)),
        ],
        out_specs=pl.BlockSpec((None, nq, DSA_HEADS * DSA_DH), lambda bi, qb: (bi, qb, 0)),
        out_shape=jax.ShapeDtypeStruct((b, s, DSA_HEADS * DSA_DH), BF16),
        scratch_shapes=[
            pltpu.VMEM((2, s, LANES), BF16),
            pltpu.VMEM((DSA_KV_HEADS, s // DSA_KC, DSA_DH + DSA_ONES, DSA_KC), BF16),
            pltpu.VMEM((s, nq), F32),
            pltpu.VMEM((1, nq), I32),
            pltpu.VMEM((2, 1, nq), F32),
            pltpu.VMEM((1, nq), I32),
            pltpu.VMEM((DSA_KV_HEADS, 1, rep * nq), F32),
            pltpu.VMEM((DSA_KV_HEADS, 1, rep * nq), F32),
            pltpu.VMEM((DSA_KV_HEADS, DSA_DH + DSA_ONES, rep * nq), F32),
            pltpu.VMEM((DSA_KV_HEADS, DSA_KC, rep * nq), F32),
        ],
        compiler_params=_cparams(("arbitrary", "arbitrary")),
        name="dsa",
    )(proj, proj, small, proj, proj, small, ikg)


FFN_CHUNK = 256


def _mix_ffn_body(x_ref, ga_ref, gb_ref, ba_ref, bb_ref, og_ref, od_ref, wo_ref,
                  g2_ref, wup_ref, cw_ref, cb_ref, wdn_ref, gf_ref, o_ref,
                  carry_ref, xx_ref, *, tm):
    t = pl.program_id(1)

    @pl.when(t == 0)
    def _():
        carry_ref[...] = jnp.zeros_like(carry_ref)

    sa = _sigmoid(ga_ref[...].astype(F32) + ba_ref[...])
    sb = _sigmoid(gb_ref[...].astype(F32) + bb_ref[...])
    mixed = sa * og_ref[...].astype(F32) + sb * od_ref[...].astype(F32)
    h = x_ref[...] + _dot(mixed.astype(BF16), wo_ref[...])
    hn =((h * lax.rsqrt(jnp.mean(h * h, axis=-1, keepdims=True) + EPS)) * g2_ref[...]).astype(BF16)

    def up_cols(c0):
        cs = slice(c0, c0 + FFN_CHUNK)
        up = _dot(hn, wup_ref[:, cs])
        xx_ref[0:SUBLANES, cs] = carry_ref[:, cs]
        xx_ref[SUBLANES:SUBLANES + tm, cs] = up
        carry_ref[:, cs] = up[tm - SUBLANES:tm, :]

    def conv_cols(c0):
        cs = slice(c0, c0 + FFN_CHUNK)
        w = cw_ref[:, cs]
        y = cb_ref[:, cs] + w[0:1, :] * xx_ref[SUBLANES - (FFN_CONV - 1):SUBLANES - (FFN_CONV - 1) + tm, cs]
        for j in range(1, FFN_CONV):
            s0 = SUBLANES - (FFN_CONV - 1) + j
            y = y + w[j:j + 1, :] * xx_ref[s0:s0 + tm, cs]
        return y

    n_chunk = D_FF // FFN_CHUNK
    up_cols(0)
    up_cols(D_FF)
    acc = jnp.zeros((tm, D_MODEL), F32)
    for c in range(n_chunk):
        if c + 1 < n_chunk:
            up_cols((c + 1) * FFN_CHUNK)
            up_cols(D_FF + (c + 1) * FFN_CHUNK)
        gate = conv_cols(c * FFN_CHUNK)
        val = conv_cols(D_FF + c * FFN_CHUNK)
        act = (_silu(gate) * val).astype(BF16)
        acc = acc + _dot(act, wdn_ref[c * FFN_CHUNK:(c + 1) * FFN_CHUNK, :])
    h2 = h + acc
    o_ref[...] = (h2 * lax.rsqrt(jnp.mean(h2 * h2, axis=-1, keepdims=True) + EPS)) * gf_ref[...]


def _mix_ffn(x, proj, bias_a, bias_b, o_gdn, o_dsa, w_out, g2, w_up, conv_w, conv_b, w_down, gf, *, tm=256):
    b, s, _ = x.shape
    const = lambda bi, t: (0, 0)
    row = lambda bi, t: (bi, t, 0)
    return pl.pallas_call(
        functools.partial(_mix_ffn_body, tm=tm),
        grid=(b, s // tm),
        in_specs=[
            pl.BlockSpec((None, tm, D_MODEL), row),
            pl.BlockSpec((None, tm, D_MODEL), lambda bi, t: (bi, t, COL_GA // D_MODEL)),
            pl.BlockSpec((None, tm, D_MODEL), lambda bi, t: (bi, t, COL_GB // D_MODEL)),
            pl.BlockSpec((1, D_MODEL), const),
            pl.BlockSpec((1, D_MODEL), const),
            pl.BlockSpec((None, tm, D_MODEL), row),
            pl.BlockSpec((None, tm, D_MODEL), row),
            pl.BlockSpec((D_MODEL, D_MODEL), const, pipeline_mode=pl.Buffered(1)),
            pl.BlockSpec((1, D_MODEL), const),
            pl.BlockSpec((D_MODEL, 2 * D_FF), const, pipeline_mode=pl.Buffered(1)),
            pl.BlockSpec((FFN_CONV, 2 * D_FF), const),
            pl.BlockSpec((1, 2 * D_FF), const),
            pl.BlockSpec((D_FF, D_MODEL), const, pipeline_mode=pl.Buffered(1)),
            pl.BlockSpec((1, D_MODEL), const),
        ],
        out_specs=pl.BlockSpec((None, tm, D_MODEL), lambda bi, t: (bi, t, 0)),
        out_shape=jax.ShapeDtypeStruct((b, s, D_MODEL), F32),
        scratch_shapes=[
            pltpu.VMEM((SUBLANES, 2 * D_FF), F32),
            pltpu.VMEM((SUBLANES + tm, 2 * D_FF), F32),
        ],
        compiler_params=_cparams(("arbitrary", "arbitrary")),
        name="mix_ffn",
    )(x, proj, proj, bias_a, bias_b, o_gdn, o_dsa, w_out, g2, w_up, conv_w, conv_b, w_down, gf)


def _rearranged_w_in(w):
    o = np.cumsum([0, 3072, 1024, 8, 8, 1024, 256, 256, 512, 64, 8, 1024, 1024])
    g_qkv, g_z, g_a, g_b, d_q, d_k, d_v, i_q, i_k, i_w, gate_a, gate_b = (
        w[:, int(o[i]):int(o[i + 1])] for i in range(12))
    big = jnp.concatenate([g_qkv, g_z, d_q, d_k, d_v, i_q, gate_a, gate_b], axis=1).astype(BF16)
    pad = jnp.zeros((w.shape[0], LANES - (IDX_DIM + 3 * 8)), w.dtype)
    small = jnp.concatenate([i_k, g_a, g_b, i_w, pad], axis=1).astype(BF16)
    return big, small


def _lane_vec(v, offset):
    return jnp.zeros((1, LANES), F32).at[0, offset:offset + v.shape[0]].set(v.astype(F32))


def kernel(x, norm1_g, w_in, gdn_conv_w, gdn_a_log, gdn_dt_bias, gdn_norm_g, idx_k_norm_g,
           branch_gate_b, w_out, norm2_g, ffn_w_up, ffn_conv_w, ffn_conv_b, ffn_w_down, final_g):
    b, s, d = x.shape
    depth = norm1_g.shape[0]
    assert depth == 1, "the final RMSNorm is fused into the (single) channel-mixer call"
    h = x
    for l in range(depth):
        w_big, w_small = _rearranged_w_in(w_in[l])
        proj2, small2 = _inproj(h.reshape(b * s, d), norm1_g[l][None, :], w_big, w_small)
        proj = proj2.reshape(b, s, N_BIG)
        small = small2.reshape(b, s, LANES)
        o_gdn = _gdn(proj, small, gdn_conv_w[l], _lane_vec(gdn_a_log[l], SM_A),
                     _lane_vec(gdn_dt_bias[l], SM_A), gdn_norm_g[l][None, :].astype(F32))
        o_dsa = _dsa(proj, small, _lane_vec(idx_k_norm_g[l], SM_IK))
        gb = branch_gate_b[l]
        h = _mix_ffn(h, proj, gb[None, :D_MODEL], gb[None, D_MODEL:], o_gdn, o_dsa, w_out[l].astype(BF16),
                     norm2_g[l][None, :], ffn_w_up[l].astype(BF16), ffn_conv_w[l],
                     ffn_conv_b[l][None, :], ffn_w_down[l].astype(BF16), final_g[None, :])
    return h
```

```python
import functools

import jax
import jax.numpy as jnp
import numpy as np
from jax import lax
from jax.experimental import pallas as pl
from jax.experimental.pallas import tpu as pltpu

F32 = jnp.float32
BF16 = jnp.bfloat16
I32 = jnp.int32

LANES = 128
SUBLANES = 8
VMEM_LIMIT = 56 * 1024 * 1024

D_MODEL = 1024
GDN_HEADS = 8
GDN_D = 128
GDN_CONV = 4
DSA_HEADS = 8
DSA_KV_HEADS = 2
DSA_DH = 128
IDX_HEADS = 8
IDX_DIM = 64
TOPK_MAX = 256
D_FF = 2816
FFN_CONV = 3
EPS = 1e-6

COL_QKV = 0
COL_Z = 3072
COL_DQ = 4096
COL_DK = 5120
COL_DV = 5376
COL_IQ = 5632
COL_GA = 6144
COL_GB = 7168
N_BIG = 8192
SM_IK = 0
SM_A = 64
SM_B = 72
SM_IW = 80

NEG_BIG = -1e30


def _cparams(sem):
    return pltpu.CompilerParams(dimension_semantics=sem, vmem_limit_bytes=VMEM_LIMIT)


def _dot(a, b):
    return jnp.dot(a, b, preferred_element_type=F32)


def _dot_nt(a, b):
    return lax.dot_general(a, b, (((1,), (1,)), ((), ())), preferred_element_type=F32)


def _dot_tn(a, b):
    return lax.dot_general(a, b, (((0,), (0,)), ((), ())), preferred_element_type=F32)


def _split3(x):
    h = x.astype(BF16)
    r = x - h.astype(F32)
    m = r.astype(BF16)
    l = (r - m.astype(F32)).astype(BF16)
    return h, m, l


def _sigmoid(x):
    return 1.0 / (1.0 + jnp.exp(-x))


def _silu(x):
    h = 0.5 * x
    return h + h * jnp.tanh(h)


def _softplus(x):
    return jnp.maximum(x, 0.0) + jnp.log(1.0 + jnp.exp(-jnp.abs(x)))


def _inproj_body(x_ref, g_ref, w_ref, ws_ref, proj_ref, small_ref, *, col_chunk):
    x = x_ref[...]
    ms = jnp.mean(x * x, axis=-1, keepdims=True)
    u = ((x * lax.rsqrt(ms + EPS)) * g_ref[...]).astype(BF16)
    for c in range(N_BIG // col_chunk):
        sl = slice(c * col_chunk, (c + 1) * col_chunk)
        proj_ref[:, sl] = _dot(u, w_ref[:, sl]).astype(BF16)
    small_ref[...] = _dot(u, ws_ref[...])


def _inproj(x2, g, w_big, w_small, *, tm=256):
    n = x2.shape[0]
    const = lambda i: (0, 0)
    return pl.pallas_call(
        functools.partial(_inproj_body, col_chunk=1024),
        grid=(n // tm,),
        in_specs=[
            pl.BlockSpec((tm, D_MODEL), lambda i: (i, 0)),
            pl.BlockSpec((1, D_MODEL), const),
            pl.BlockSpec((D_MODEL, N_BIG), const, pipeline_mode=pl.Buffered(1)),
            pl.BlockSpec((D_MODEL, LANES), const, pipeline_mode=pl.Buffered(1)),
        ],
        out_specs=[
            pl.BlockSpec((tm, N_BIG), lambda i: (i, 0)),
            pl.BlockSpec((tm, LANES), lambda i: (i, 0)),
        ],
        out_shape=[
            jax.ShapeDtypeStruct((n, N_BIG), BF16),
            jax.ShapeDtypeStruct((n, LANES), F32),
        ],
        compiler_params=_cparams(("arbitrary",)),
        name="inproj",
    )(x2, g, w_big, w_small)


GDN_C = 128


def _tri_inverse(a_list):
    n = a_list[0].shape[0]
    row = lax.broadcasted_iota(I32, (n, n), 0)
    col = lax.broadcasted_iota(I32, (n, n), 1)
    eye = jnp.where(row == col, 1.0, 0.0).astype(F32)

    def same_block(size):
        sh = size.bit_length() - 1
        return (row >> sh) == (col >> sh)

    def square(ps):
        return [_dot(p, p).astype(BF16) for p in ps]

    def grow(ts, ps):
        return [t + _dot(t.astype(BF16), p) for t, p in zip(ts, ps)]

    p1 = [jnp.where(same_block(16), a, 0.0) for a in a_list]
    p2s = square([p.astype(BF16) for p in p1])
    ts = [eye - p for p in p1]
    p4s = square(p2s)
    ts = grow(ts, p2s)
    p8s = square(p4s)
    ts = grow(ts, p4s)
    ts = grow(ts, p8s)
    size = 16
    while size < n:
        in_pair = same_block(2 * size) & jnp.logical_not(same_block(size))
        offs = [jnp.where(in_pair, a, 0.0).astype(BF16) for a in a_list]
        tss = [t.astype(BF16) for t in ts]
        mids = [_dot(t2, o).astype(BF16) for t2, o in zip(tss, offs)]
        ts = [t - _dot(m, t2) for t, m, t2 in zip(ts, mids, tss)]
        size *= 2
    return ts


def _gdn_body(qp_ref, kp_ref, vp_ref, z_ref, sm_ref, wq_ref, wk_ref, wv_ref,
              alog_ref, dtb_ref, ng_ref, o_ref,
              state_ref, halo_ref, xx_ref, *, tb, hg):
    hgi = pl.program_id(1)
    t = pl.program_id(2)

    @pl.when(t == 0)
    def _():
        state_ref[...] = jnp.zeros_like(state_ref)
        halo_ref[...] = jnp.zeros_like(halo_ref)

    def conv_silu(idx, x_ref, w_ref):
        x = x_ref[...].astype(F32)
        xx_ref[0:SUBLANES, :] = halo_ref[idx]
        xx_ref[SUBLANES:SUBLANES + tb, :] = x
        halo_ref[idx] = x[tb - SUBLANES:tb, :]
        w = w_ref[...]
        y = w[GDN_CONV - 1:GDN_CONV, :] * x
        for j in range(GDN_CONV - 1):
            s0 = SUBLANES - (GDN_CONV - 1) + j
            y = y + w[j:j + 1, :] * xx_ref[s0:s0 + tb, :]
        return _silu(y)

    q_all = conv_silu(0, qp_ref, wq_ref)
    k_all = conv_silu(1, kp_ref, wk_ref)
    v_all = conv_silu(2, vp_ref, wv_ref)

    sm = sm_ref[...]
    ld_all = -jnp.exp(alog_ref[...]) * _softplus(sm + dtb_ref[...])
    beta_all = _sigmoid(sm)

    c = GDN_C
    row = lax.broadcasted_iota(I32, (c, c), 0)
    col = lax.broadcasted_iota(I32, (c, c), 1)
    lane = lax.broadcasted_iota(I32, (c, LANES), 1)
    tri_incl = jnp.where(row >= col, 1.0, 0.0).astype(BF16)
    ng = ng_ref[...]

    for ci in range(tb // c):
        rs = slice(ci * c, (ci + 1) * c)
        l1, l2, l3 = _split3(ld_all[rs])
        g_all = _dot(tri_incl, l1) + (_dot(tri_incl, l2) + _dot(tri_incl, l3))
        heads = range(hg)
        cols = [slice(j * GDN_D, (j + 1) * GDN_D) for j in heads]
        gc, bc, gamma, eg, g_last, qc, kc, kb, kc16 = ([None] * hg for _ in range(9))
        for j in heads:
            h = hgi * hg + j
            gcol = jnp.sum(jnp.where(lane == SM_A + h, g_all, 0.0), axis=-1, keepdims=True)
            bc[j] = jnp.sum(jnp.where(lane == SM_B + h, beta_all[rs], 0.0), axis=-1, keepdims=True)
            gc[j] = jnp.broadcast_to(gcol, (c, c))
            gamma[j] = jnp.exp(jnp.where(row >= col, gc[j] - gc[j].T, -jnp.inf))
            eg[j] = jnp.exp(gc[j])
            g_last[j] = gc[j][c - 1:c, :]
            q_h = q_all[rs, cols[j]]
            k_h = k_all[rs, cols[j]]
            qc[j] = q_h * (lax.rsqrt(jnp.sum(q_h * q_h, axis=-1, keepdims=True) + EPS) * (GDN_D ** -0.5))
            kc[j] = k_h * lax.rsqrt(jnp.sum(k_h * k_h, axis=-1, keepdims=True) + EPS)
            kb[j] = kc[j] * bc[j]
            kc16[j] = kc[j].astype(BF16)
        a = [jnp.where(row > col, _dot_nt(kb[j].astype(BF16), kc16[j]) * gamma[j], 0.0) for j in heads]
        att = [(_dot_nt(qc[j].astype(BF16), kc16[j]) * gamma[j]).astype(BF16) for j in heads]
        tinv = [t.astype(BF16) for t in _tri_inverse(a)]
        u = [_dot(tinv[j], (v_all[rs, cols[j]] * bc[j]).astype(BF16)) for j in heads]
        w = [_dot(tinv[j], (kb[j] * eg[j]).astype(BF16)).astype(BF16) for j in heads]
        qg = [(qc[j] * eg[j]).astype(BF16) for j in heads]
        kd = [(kc[j] * jnp.exp(g_last[j] - gc[j])).astype(BF16) for j in heads]

        s_old = [state_ref[j] for j in heads]
        s16 = [s.astype(BF16) for s in s_old]
        v_new16 = [(u[j] - _dot(w[j], s16[j])).astype(BF16) for j in heads]
        o = [_dot(qg[j], s16[j]) + _dot(att[j], v_new16[j]) for j in heads]
        for j in heads:
            state_ref[j] = s_old[j] * jnp.exp(g_last[j]) + _dot_tn(kd[j], v_new16[j])
        for j in heads:
            y = o[j] * lax.rsqrt(jnp.mean(o[j] * o[j], axis=-1, keepdims=True) + EPS) * ng
            o_ref[rs, cols[j]] = (y * _silu(z_ref[rs, cols[j]].astype(F32))).astype(o_ref.dtype)


def _gdn(proj, small, conv_w, alog_v, dtb_v, ng, *, tb=256, hg=8):
    b, s, _ = proj.shape
    nh = GDN_HEADS
    w = hg * GDN_D
    ng_blocks = nh // hg
    hq = COL_QKV // w
    hz = COL_Z // w
    return pl.pallas_call(
        functools.partial(_gdn_body, tb=tb, hg=hg),
        grid=(b, ng_blocks, s // tb),
        in_specs=[
            pl.BlockSpec((None, tb, w), lambda bi, g, t: (bi, t, hq + g)),
            pl.BlockSpec((None, tb, w), lambda bi, g, t: (bi, t, hq + ng_blocks + g)),
            pl.BlockSpec((None, tb, w), lambda bi, g, t: (bi, t, hq + 2 * ng_blocks + g)),
            pl.BlockSpec((None, tb, w), lambda bi, g, t: (bi, t, hz + g)),
            pl.BlockSpec((None, tb, LANES), lambda bi, g, t: (bi, t, 0)),
            pl.BlockSpec((GDN_CONV, w), lambda bi, g, t: (0, g)),
            pl.BlockSpec((GDN_CONV, w), lambda bi, g, t: (0, ng_blocks + g)),
            pl.BlockSpec((GDN_CONV, w), lambda bi, g, t: (0, 2 * ng_blocks + g)),
            pl.BlockSpec((1, LANES), lambda bi, g, t: (0, 0)),
            pl.BlockSpec((1, LANES), lambda bi, g, t: (0, 0)),
            pl.BlockSpec((1, LANES), lambda bi, g, t: (0, 0)),
        ],
        out_specs=pl.BlockSpec((None, tb, w), lambda bi, g, t: (bi, t, g)),
        out_shape=jax.ShapeDtypeStruct((b, s, nh * GDN_D), BF16),
        scratch_shapes=[
            pltpu.VMEM((hg, GDN_D, GDN_D), F32),
            pltpu.VMEM((3, SUBLANES, w), F32),
            pltpu.VMEM((SUBLANES + tb, w), F32),
        ],
        compiler_params=_cparams(("arbitrary", "arbitrary", "arbitrary")),
        name="gdn",
    )(proj, proj, proj, proj, small, conv_w, conv_w, conv_w, alog_v, dtb_v, ng)


DSA_QB = 128
DSA_KC = 512
DSA_ONES = 16
_KEY_NEG_INF = np.int32(np.array(0xFF800000, np.uint32).view(np.int32) ^ np.int32(0x7FFFFFFF))
_KEY_POS_INF_P1 = np.int32(0x7F800001)


def _key_to_f32(key):
    bits = key ^ ((key >> 31) & jnp.int32(0x7FFFFFFF))
    return lax.bitcast_convert_type(bits, F32)


def _tree_sum(xs):
    xs = list(xs)
    while len(xs) > 1:
        nxt = [xs[i] + xs[i + 1] for i in range(0, len(xs) - 1, 2)]
        if len(xs) % 2:
            nxt.append(xs[-1])
        xs = nxt
    return xs[0]


def _count_rows(hit):
    rows, n = hit.shape
    parts = hit.reshape(rows // SUBLANES, SUBLANES, n)
    return _tree_sum([parts[i] for i in range(rows // SUBLANES)])


def _dsa_body(q_ref, iq_ref, smq_ref, k_ref, v_ref, smk_ref, ikg_ref, o_ref,
              ikn_ref, vt_ref, s_ref, thrkey_ref, cnt_ref, jsel_ref, m_ref, alpha_ref, acc_ref, lg_ref, *, seq, k_top):
    qb = pl.program_id(1)
    kc = DSA_KC
    nq = DSA_QB
    rep = DSA_HEADS // DSA_KV_HEADS

    @pl.when(qb == 0)
    def _():
        def prep(i, carry):
            rs = pl.ds(pl.multiple_of(i * kc, kc), kc)
            sm = smk_ref[rs, :]
            lane = lax.broadcasted_iota(I32, sm.shape, 1)
            isk = lane < IDX_DIM
            mean = jnp.sum(jnp.where(isk, sm, 0.0), axis=-1, keepdims=True) * (1.0 / IDX_DIM)
            xc = jnp.where(isk, sm - mean, 0.0)
            var = jnp.sum(xc * xc, axis=-1, keepdims=True) * (1.0 / IDX_DIM)
            y = xc * lax.rsqrt(var + EPS) * ikg_ref[...]
            ikn_ref[0, rs, :] = y.astype(BF16)
            ikn_ref[1, rs, :] = pltpu.roll(y, IDX_DIM, axis=1).astype(BF16)
            vv = v_ref[rs, :].astype(F32)
            for g in range(DSA_KV_HEADS):
                vt_ref[g, i, 0:DSA_DH, :] = vv[:, g * DSA_DH:(g + 1) * DSA_DH].T.astype(BF16)
                vt_ref[g, i, DSA_DH:DSA_DH + DSA_ONES, :] = jnp.ones((DSA_ONES, kc), BF16)
            return carry
        lax.fori_loop(0, seq // kc, prep, 0)

    q_lo = qb * nq
    n_kc = (q_lo + nq + kc - 1) // kc
    qpos = q_lo + lax.broadcasted_iota(I32, (1, nq), 1)

    iq = iq_ref[...]
    n_pair = IDX_HEADS // 2
    iq_t = jnp.concatenate([iq[:, p * LANES:(p + 1) * LANES].astype(F32).T for p in range(n_pair)],
                           axis=1).astype(BF16)
    iw_t = smq_ref[...].T * ((IDX_HEADS ** -0.5) * (IDX_DIM ** -0.5))

    def score_chunk(i, carry):
        rs = pl.ds(pl.multiple_of(i * kc, kc), kc)
        lo = _dot(ikn_ref[0, rs, :], iq_t)
        hi = _dot(ikn_ref[1, rs, :], iq_t)
        sc = jnp.zeros((kc, nq), F32)
        for p in range(n_pair):
            cs = slice(p * nq, (p + 1) * nq)
            sc = sc + iw_t[SM_IW + 2 * p:SM_IW + 2 * p + 1, :] * jnp.maximum(lo[:, cs], 0.0)
            sc = sc + iw_t[SM_IW + 2 * p + 1:SM_IW + 2 * p + 2, :] * jnp.maximum(hi[:, cs], 0.0)
        kpos = i * kc + lax.broadcasted_iota(I32, (kc, nq), 0)
        s_ref[rs, :] = jnp.where(kpos <= qpos, sc + 0.0, -jnp.inf)
        return carry
    lax.fori_loop(0, n_kc, score_chunk, 0)

    kf = jnp.float32(k_top)

    def midpoint(lo, hi):
        return (lo >> 1) + (hi >> 1) + (lo & hi & 1)

    def bisect_static(rows):
        pieces = [slice(r0, min(r0 + kc, rows)) for r0 in range(0, rows, kc)]

        def count(hit_fn):
            parts = [_count_rows(jnp.where(hit_fn(s_ref[rs, :]), 1.0, 0.0)) for rs in pieces]
            return jnp.sum(_tree_sum(parts), axis=0, keepdims=True)

        def count_ge(thr):
            return count(lambda s: s >= thr)

        def bisect(_, carry):
            lo, hi = carry
            mid = midpoint(lo, hi)
            ok = count_ge(_key_to_f32(mid)) >= kf
            return jnp.where(ok, mid, lo), jnp.where(ok, hi, mid)
        lo0 = jnp.full((1, nq), _KEY_NEG_INF, I32)
        hi0 = jnp.full((1, nq), _KEY_POS_INF_P1, I32)
        lo, _ = lax.fori_loop(0, 32, bisect, (lo0, hi0))
        thrkey_ref[...] = lo
        thr = _key_to_f32(lo)
        cnt_ref[0] = count(lambda s: s > thr)
        cnt_ref[1] = count_ge(thr)

    for j in range(seq // nq):
        pl.when(qb == j)(functools.partial(bisect_static, (j + 1) * nq))
    thr = _key_to_f32(thrkey_ref[...])

    n_gt = cnt_ref[0]
    n_ge = cnt_ref[1]
    need = kf - n_gt
    jsel_ref[...] = jnp.full((1, nq), seq, I32)

    @pl.when(jnp.max(n_ge) > kf)
    def _():
        def count_eq_upto(jmax):
            def body(i, cnt):
                rs = pl.ds(pl.multiple_of(i * kc, kc), kc)
                kpos = i * kc + lax.broadcasted_iota(I32, (kc, nq), 0)
                hit = jnp.where(s_ref[rs, :] == thr, jnp.where(kpos <= jmax, 1.0, 0.0), 0.0)
                return cnt + _count_rows(hit)
            cnt = lax.fori_loop(0, n_kc, body, jnp.zeros((SUBLANES, nq), F32))
            return jnp.sum(cnt, axis=0, keepdims=True)

        def jbisect(_, carry):
            lo_j, hi_j = carry
            mid = (lo_j + hi_j) >> 1
            ok = count_eq_upto(mid) >= need
            return jnp.where(ok, lo_j, mid), jnp.where(ok, mid, hi_j)
        n_iter = int(np.ceil(np.log2(seq))) + 1
        lo_j0 = jnp.full((1, nq), -1, I32)
        hi_j0 = jnp.full((1, nq), seq - 1, I32)
        _, hi_j = lax.fori_loop(0, n_iter, jbisect, (lo_j0, hi_j0))
        jsel_ref[...] = hi_j

    jsel = jsel_ref[...]

    def selection_bias(i):
        s = s_ref[pl.ds(pl.multiple_of(i * kc, kc), kc), :]
        kpos = i * kc + lax.broadcasted_iota(I32, (kc, nq), 0)
        tie = jnp.where(s == thr, jnp.where(kpos <= jsel, 0.0, NEG_BIG), NEG_BIG)
        sel = jnp.where(s > thr, 0.0, tie)
        return jnp.where(kpos <= qpos, sel, NEG_BIG)

    qall = q_ref[...].astype(F32) * (DSA_DH ** -0.5 * np.log2(np.e))
    q_t = [jnp.concatenate([qall[:, (g * rep + r) * DSA_DH:(g * rep + r + 1) * DSA_DH].T
                            for r in range(rep)], axis=1).astype(BF16) for g in range(DSA_KV_HEADS)]
    m_ref[...] = jnp.full(m_ref.shape, NEG_BIG, F32)
    acc_ref[...] = jnp.zeros(acc_ref.shape, F32)

    def logits_phase(i):
        rs = pl.ds(pl.multiple_of(i * kc, kc), kc)
        bias_w = jnp.concatenate([selection_bias(i)] * rep, axis=1)
        for g in range(DSA_KV_HEADS):
            logit = _dot(k_ref[rs, g * DSA_DH:(g + 1) * DSA_DH], q_t[g]) + bias_w
            lg_ref[g] = logit
            m_old = m_ref[g]
            m_new = jnp.maximum(m_old, jnp.max(logit, axis=0, keepdims=True))
            alpha_ref[g] = jnp.exp2(m_old - m_new)
            m_ref[g] = m_new

    def value_phase(i):
        for g in range(DSA_KV_HEADS):
            p = jnp.exp2(lg_ref[g] - m_ref[g]).astype(BF16)
            acc_ref[g] = acc_ref[g] * alpha_ref[g] + _dot(vt_ref[g, i], p)

    logits_phase(0)

    def attn_chunk(i, carry):
        value_phase(i - 1)
        logits_phase(i)
        return carry
    lax.fori_loop(1, n_kc, attn_chunk, 0)
    value_phase(n_kc - 1)

    for g in range(DSA_KV_HEADS):
        acc = acc_ref[g]
        ot = acc[0:DSA_DH] / acc[DSA_DH:DSA_DH + 1]
        for r in range(rep):
            hh = g * rep + r
            o_ref[:, hh * DSA_DH:(hh + 1) * DSA_DH] = ot[:, r * nq:(r + 1) * nq].T.astype(o_ref.dtype)


def _dsa(proj, small, ikg):
    b, s, _ = proj.shape
    nq = DSA_QB
    k_top = min(TOPK_MAX, s // 4)
    rep = DSA_HEADS // DSA_KV_HEADS
    return pl.pallas_call(
        functools.partial(_dsa_body, seq=s, k_top=k_top),
        grid=(b, s // nq),
        in_specs=[
            pl.BlockSpec((None, nq, DSA_HEADS * DSA_DH), lambda bi, qb: (bi, qb, COL_DQ // (DSA_HEADS * DSA_DH))),
            pl.BlockSpec((None, nq, IDX_HEADS * IDX_DIM), lambda bi, qb: (bi, qb, COL_IQ // (IDX_HEADS * IDX_DIM))),
            pl.BlockSpec((None, nq, LANES), lambda bi, qb: (bi, qb, 0)),
            pl.BlockSpec((None, s, DSA_KV_HEADS * DSA_DH), lambda bi, qb: (bi, 0, COL_DK // (DSA_KV_HEADS * DSA_DH))),
            pl.BlockSpec((None, s, DSA_KV_HEADS * DSA_DH), lambda bi, qb: (bi, 0, COL_DV // (DSA_KV_HEADS * DSA_DH))),
            pl.BlockSpec((None, s, LANES), lambda bi, qb: (bi, 0, 0)),
            pl.BlockSpec((1, LANES), lambda bi, qb: (0, 0)),
        ],
        out_specs=pl.BlockSpec((None, nq, DSA_HEADS * DSA_DH), lambda bi, qb: (bi, qb, 0)),
        out_shape=jax.ShapeDtypeStruct((b, s, DSA_HEADS * DSA_DH), BF16),
        scratch_shapes=[
            pltpu.VMEM((2, s, LANES), BF16),
            pltpu.VMEM((DSA_KV_HEADS, s // DSA_KC, DSA_DH + DSA_ONES, DSA_KC), BF16),
            pltpu.VMEM((s, nq), F32),
            pltpu.VMEM((1, nq), I32),
            pltpu.VMEM((2, 1, nq), F32),
            pltpu.VMEM((1, nq), I32),
            pltpu.VMEM((DSA_KV_HEADS, 1, rep * nq), F32),
            pltpu.VMEM((DSA_KV_HEADS, 1, rep * nq), F32),
            pltpu.VMEM((DSA_KV_HEADS, DSA_DH + DSA_ONES, rep * nq), F32),
            pltpu.VMEM((DSA_KV_HEADS, DSA_KC, rep * nq), F32),
        ],
        compiler_params=_cparams(("arbitrary", "arbitrary")),
        name="dsa",
    )(proj, proj, small, proj, proj, small, ikg)


FFN_CHUNK = 256


def _mix_ffn_body(x_ref, ga_ref, gb_ref, ba_ref, bb_ref, og_ref, od_ref, wo_ref,
                  g2_ref, wup_ref, cw_ref, cb_ref, wdn_ref, gf_ref, o_ref,
                  carry_ref, xx_ref, *, tm):
    t = pl.program_id(1)

    @pl.when(t == 0)
    def _():
        carry_ref[...] = jnp.zeros_like(carry_ref)

    sa = _sigmoid(ga_ref[...].astype(F32) + ba_ref[...])
    sb = _sigmoid(gb_ref[...].astype(F32) + bb_ref[...])
    mixed = sa * og_ref[...].astype(F32) + sb * od_ref[...].astype(F32)
    h = x_ref[...] + _dot(mixed.astype(BF16), wo_ref[...])
    hn =((h * lax.rsqrt(jnp.mean(h * h, axis=-1, keepdims=True) + EPS)) * g2_ref[...]).astype(BF16)

    def up_cols(c0):
        cs = slice(c0, c0 + FFN_CHUNK)
        up = _dot(hn, wup_ref[:, cs])
        xx_ref[0:SUBLANES, cs] = carry_ref[:, cs]
        xx_ref[SUBLANES:SUBLANES + tm, cs] = up
        carry_ref[:, cs] = up[tm - SUBLANES:tm, :]

    def conv_cols(c0):
        cs = slice(c0, c0 + FFN_CHUNK)
        w = cw_ref[:, cs]
        y = cb_ref[:, cs] + w[0:1, :] * xx_ref[SUBLANES - (FFN_CONV - 1):SUBLANES - (FFN_CONV - 1) + tm, cs]
        for j in range(1, FFN_CONV):
            s0 = SUBLANES - (FFN_CONV - 1) + j
            y = y + w[j:j + 1, :] * xx_ref[s0:s0 + tm, cs]
        return y

    n_chunk = D_FF // FFN_CHUNK
    up_cols(0)
    up_cols(D_FF)
    acc = jnp.zeros((tm, D_MODEL), F32)
    for c in range(n_chunk):
        if c + 1 < n_chunk:
            up_cols((c + 1) * FFN_CHUNK)
            up_cols(D_FF + (c + 1) * FFN_CHUNK)
        gate = conv_cols(c * FFN_CHUNK)
        val = conv_cols(D_FF + c * FFN_CHUNK)
        act = (_silu(gate) * val).astype(BF16)
        acc = acc + _dot(act, wdn_ref[c * FFN_CHUNK:(c + 1) * FFN_CHUNK, :])
    h2 = h + acc
    o_ref[...] = (h2 * lax.rsqrt(jnp.mean(h2 * h2, axis=-1, keepdims=True) + EPS)) * gf_ref[...]


def _mix_ffn(x, proj, bias_a, bias_b, o_gdn, o_dsa, w_out, g2, w_up, conv_w, conv_b, w_down, gf, *, tm=256):
    b, s, _ = x.shape
    const = lambda bi, t: (0, 0)
    row = lambda bi, t: (bi, t, 0)
    return pl.pallas_call(
        functools.partial(_mix_ffn_body, tm=tm),
        grid=(b, s // tm),
        in_specs=[
            pl.BlockSpec((None, tm, D_MODEL), row),
            pl.BlockSpec((None, tm, D_MODEL), lambda bi, t: (bi, t, COL_GA // D_MODEL)),
            pl.BlockSpec((None, tm, D_MODEL), lambda bi, t: (bi, t, COL_GB // D_MODEL)),
            pl.BlockSpec((1, D_MODEL), const),
            pl.BlockSpec((1, D_MODEL), const),
            pl.BlockSpec((None, tm, D_MODEL), row),
            pl.BlockSpec((None, tm, D_MODEL), row),
            pl.BlockSpec((D_MODEL, D_MODEL), const, pipeline_mode=pl.Buffered(1)),
            pl.BlockSpec((1, D_MODEL), const),
            pl.BlockSpec((D_MODEL, 2 * D_FF), const, pipeline_mode=pl.Buffered(1)),
            pl.BlockSpec((FFN_CONV, 2 * D_FF), const),
            pl.BlockSpec((1, 2 * D_FF), const),
            pl.BlockSpec((D_FF, D_MODEL), const, pipeline_mode=pl.Buffered(1)),
            pl.BlockSpec((1, D_MODEL), const),
        ],
        out_specs=pl.BlockSpec((None, tm, D_MODEL), lambda bi, t: (bi, t, 0)),
        out_shape=jax.ShapeDtypeStruct((b, s, D_MODEL), F32),
        scratch_shapes=[
            pltpu.VMEM((SUBLANES, 2 * D_FF), F32),
            pltpu.VMEM((SUBLANES + tm, 2 * D_FF), F32),
        ],
        compiler_params=_cparams(("arbitrary", "arbitrary")),
        name="mix_ffn",
    )(x, proj, proj, bias_a, bias_b, o_gdn, o_dsa, w_out, g2, w_up, conv_w, conv_b, w_down, gf)


def _rearranged_w_in(w):
    o = np.cumsum([0, 3072, 1024, 8, 8, 1024, 256, 256, 512, 64, 8, 1024, 1024])
    g_qkv, g_z, g_a, g_b, d_q, d_k, d_v, i_q, i_k, i_w, gate_a, gate_b = (
        w[:, int(o[i]):int(o[i + 1])] for i in range(12))
    big = jnp.concatenate([g_qkv, g_z, d_q, d_k, d_v, i_q, gate_a, gate_b], axis=1).astype(BF16)
    pad = jnp.zeros((w.shape[0], LANES - (IDX_DIM + 3 * 8)), w.dtype)
    small = jnp.concatenate([i_k, g_a, g_b, i_w, pad], axis=1).astype(BF16)
    return big, small


def _lane_vec(v, offset):
    return jnp.zeros((1, LANES), F32).at[0, offset:offset + v.shape[0]].set(v.astype(F32))


def kernel(x, norm1_g, w_in, gdn_conv_w, gdn_a_log, gdn_dt_bias, gdn_norm_g, idx_k_norm_g,
           branch_gate_b, w_out, norm2_g, ffn_w_up, ffn_conv_w, ffn_conv_b, ffn_w_down, final_g):
    b, s, d = x.shape
    depth = norm1_g.shape[0]
    assert depth == 1, "the final RMSNorm is fused into the (single) channel-mixer call"
    h = x
    for l in range(depth):
        w_big, w_small = _rearranged_w_in(w_in[l])
        proj2, small2 = _inproj(h.reshape(b * s, d), norm1_g[l][None, :], w_big, w_small)
        proj = proj2.reshape(b, s, N_BIG)
        small = small2.reshape(b, s, LANES)
        o_gdn = _gdn(proj, small, gdn_conv_w[l], _lane_vec(gdn_a_log[l], SM_A),
                     _lane_vec(gdn_dt_bias[l], SM_A), gdn_norm_g[l][None, :].astype(F32))
        o_dsa = _dsa(proj, small, _lane_vec(idx_k_norm_g[l], SM_IK))
        gb = branch_gate_b[l]
        h = _mix_ffn(h, proj, gb[None, :D_MODEL], gb[None, D_MODEL:], o_gdn, o_dsa, w_out[l].astype(BF16),
                     norm2_g[l][None, :], ffn_w_up[l].astype(BF16), ffn_conv_w[l],
                     ffn_conv_b[l][None, :], ffn_w_down[l].astype(BF16), final_g[None, :])
    return h
```

```python
import functools

import jax
import jax.numpy as jnp
import numpy as np
from jax import lax
from jax.experimental import pallas as pl
from jax.experimental.pallas import tpu as pltpu

F32 = jnp.float32
BF16 = jnp.bfloat16
I32 = jnp.int32

LANES = 128
SUBLANES = 8
VMEM_LIMIT = 56 * 1024 * 1024

D_MODEL = 1024
GDN_HEADS = 8
GDN_D = 128
GDN_CONV = 4
DSA_HEADS = 8
DSA_KV_HEADS = 2
DSA_DH = 128
IDX_HEADS = 8
IDX_DIM = 64
TOPK_MAX = 256
D_FF = 2816
FFN_CONV = 3
EPS = 1e-6

COL_QKV = 0
COL_Z = 3072
COL_DQ = 4096
COL_DK = 5120
COL_DV = 5376
COL_IQ = 5632
COL_GA = 6144
COL_GB = 7168
N_BIG = 8192
SM_IK = 0
SM_A = 64
SM_B = 72
SM_IW = 80

NEG_BIG = -1e30


def _cparams(sem):
    return pltpu.CompilerParams(dimension_semantics=sem, vmem_limit_bytes=VMEM_LIMIT)


def _dot(a, b):
    return jnp.dot(a, b, preferred_element_type=F32)


def _dot_nt(a, b):
    return lax.dot_general(a, b, (((1,), (1,)), ((), ())), preferred_element_type=F32)


def _dot_tn(a, b):
    return lax.dot_general(a, b, (((0,), (0,)), ((), ())), preferred_element_type=F32)


def _split3(x):
    h = x.astype(BF16)
    r = x - h.astype(F32)
    m = r.astype(BF16)
    l = (r - m.astype(F32)).astype(BF16)
    return h, m, l


def _sigmoid(x):
    return 1.0 / (1.0 + jnp.exp(-x))


def _silu(x):
    h = 0.5 * x
    return h + h * jnp.tanh(h)


def _softplus(x):
    return jnp.maximum(x, 0.0) + jnp.log(1.0 + jnp.exp(-jnp.abs(x)))


def _inproj_body(x_ref, g_ref, w_ref, ws_ref, proj_ref, small_ref, *, col_chunk):
    x = x_ref[...]
    ms = jnp.mean(x * x, axis=-1, keepdims=True)
    u = ((x * lax.rsqrt(ms + EPS)) * g_ref[...]).astype(BF16)
    for c in range(N_BIG // col_chunk):
        sl = slice(c * col_chunk, (c + 1) * col_chunk)
        proj_ref[:, sl] = _dot(u, w_ref[:, sl]).astype(BF16)
    small_ref[...] = _dot(u, ws_ref[...])


def _inproj(x2, g, w_big, w_small, *, tm=256):
    n = x2.shape[0]
    const = lambda i: (0, 0)
    return pl.pallas_call(
        functools.partial(_inproj_body, col_chunk=1024),
        grid=(n // tm,),
        in_specs=[
            pl.BlockSpec((tm, D_MODEL), lambda i: (i, 0)),
            pl.BlockSpec((1, D_MODEL), const),
            pl.BlockSpec((D_MODEL, N_BIG), const, pipeline_mode=pl.Buffered(1)),
            pl.BlockSpec((D_MODEL, LANES), const, pipeline_mode=pl.Buffered(1)),
        ],
        out_specs=[
            pl.BlockSpec((tm, N_BIG), lambda i: (i, 0)),
            pl.BlockSpec((tm, LANES), lambda i: (i, 0)),
        ],
        out_shape=[
            jax.ShapeDtypeStruct((n, N_BIG), BF16),
            jax.ShapeDtypeStruct((n, LANES), F32),
        ],
        compiler_params=_cparams(("arbitrary",)),
        name="inproj",
    )(x2, g, w_big, w_small)


GDN_C = 128


def _tri_inverse(a_list):
    n = a_list[0].shape[0]
    row = lax.broadcasted_iota(I32, (n, n), 0)
    col = lax.broadcasted_iota(I32, (n, n), 1)
    eye = jnp.where(row == col, 1.0, 0.0).astype(F32)

    def same_block(size):
        sh = size.bit_length() - 1
        return (row >> sh) == (col >> sh)

    def square(ps):
        return [_dot(p, p).astype(BF16) for p in ps]

    def grow(ts, ps):
        return [t + _dot(t.astype(BF16), p) for t, p in zip(ts, ps)]

    p1 = [jnp.where(same_block(16), a, 0.0) for a in a_list]
    p2s = square([p.astype(BF16) for p in p1])
    ts = [eye - p for p in p1]
    p4s = square(p2s)
    ts = grow(ts, p2s)
    p8s = square(p4s)
    ts = grow(ts, p4s)
    ts = grow(ts, p8s)
    size = 16
    while size < n:
        in_pair = same_block(2 * size) & jnp.logical_not(same_block(size))
        offs = [jnp.where(in_pair, a, 0.0).astype(BF16) for a in a_list]
        tss = [t.astype(BF16) for t in ts]
        mids = [_dot(t2, o).astype(BF16) for t2, o in zip(tss, offs)]
        ts = [t - _dot(m, t2) for t, m, t2 in zip(ts, mids, tss)]
        size *= 2
    return ts


def _gdn_body(qp_ref, kp_ref, vp_ref, z_ref, sm_ref, wq_ref, wk_ref, wv_ref,
              alog_ref, dtb_ref, ng_ref, o_ref,
              state_ref, halo_ref, xx_ref, *, tb, hg):
    hgi = pl.program_id(1)
    t = pl.program_id(2)

    @pl.when(t == 0)
    def _():
        state_ref[...] = jnp.zeros_like(state_ref)
        halo_ref[...] = jnp.zeros_like(halo_ref)

    def conv_silu(idx, x_ref, w_ref):
        x = x_ref[...].astype(F32)
        xx_ref[0:SUBLANES, :] = halo_ref[idx]
        xx_ref[SUBLANES:SUBLANES + tb, :] = x
        halo_ref[idx] = x[tb - SUBLANES:tb, :]
        w = w_ref[...]
        y = w[GDN_CONV - 1:GDN_CONV, :] * x
        for j in range(GDN_CONV - 1):
            s0 = SUBLANES - (GDN_CONV - 1) + j
            y = y + w[j:j + 1, :] * xx_ref[s0:s0 + tb, :]
        return _silu(y)

    q_all = conv_silu(0, qp_ref, wq_ref)
    k_all = conv_silu(1, kp_ref, wk_ref)
    v_all = conv_silu(2, vp_ref, wv_ref)

    sm = sm_ref[...]
    ld_all = -jnp.exp(alog_ref[...]) * _softplus(sm + dtb_ref[...])
    beta_all = _sigmoid(sm)

    c = GDN_C
    row = lax.broadcasted_iota(I32, (c, c), 0)
    col = lax.broadcasted_iota(I32, (c, c), 1)
    lane = lax.broadcasted_iota(I32, (c, LANES), 1)
    tri_incl = jnp.where(row >= col, 1.0, 0.0).astype(BF16)
    ng = ng_ref[...]

    for ci in range(tb // c):
        rs = slice(ci * c, (ci + 1) * c)
        l1, l2, l3 = _split3(ld_all[rs])
        g_all = _dot(tri_incl, l1) + (_dot(tri_incl, l2) + _dot(tri_incl, l3))
        heads = range(hg)
        cols = [slice(j * GDN_D, (j + 1) * GDN_D) for j in heads]
        gc, bc, gamma, eg, g_last, qc, kc, kb, kc16 = ([None] * hg for _ in range(9))
        for j in heads:
            h = hgi * hg + j
            gcol = jnp.sum(jnp.where(lane == SM_A + h, g_all, 0.0), axis=-1, keepdims=True)
            bc[j] = jnp.sum(jnp.where(lane == SM_B + h, beta_all[rs], 0.0), axis=-1, keepdims=True)
            gc[j] = jnp.broadcast_to(gcol, (c, c))
            gamma[j] = jnp.exp(jnp.where(row >= col, gc[j] - gc[j].T, -jnp.inf))
            eg[j] = jnp.exp(gc[j])
            g_last[j] = gc[j][c - 1:c, :]
            q_h = q_all[rs, cols[j]]
            k_h = k_all[rs, cols[j]]
            qc[j] = q_h * (lax.rsqrt(jnp.sum(q_h * q_h, axis=-1, keepdims=True) + EPS) * (GDN_D ** -0.5))
            kc[j] = k_h * lax.rsqrt(jnp.sum(k_h * k_h, axis=-1, keepdims=True) + EPS)
            kb[j] = kc[j] * bc[j]
            kc16[j] = kc[j].astype(BF16)
        a = [jnp.where(row > col, _dot_nt(kb[j].astype(BF16), kc16[j]) * gamma[j], 0.0) for j in heads]
        att = [(_dot_nt(qc[j].astype(BF16), kc16[j]) * gamma[j]).astype(BF16) for j in heads]
        tinv = [t.astype(BF16) for t in _tri_inverse(a)]
        u = [_dot(tinv[j], (v_all[rs, cols[j]] * bc[j]).astype(BF16)) for j in heads]
        w = [_dot(tinv[j], (kb[j] * eg[j]).astype(BF16)).astype(BF16) for j in heads]
        qg = [(qc[j] * eg[j]).astype(BF16) for j in heads]
        kd = [(kc[j] * jnp.exp(g_last[j] - gc[j])).astype(BF16) for j in heads]

        s_old = [state_ref[j] for j in heads]
        s16 = [s.astype(BF16) for s in s_old]
        v_new16 = [(u[j] - _dot(w[j], s16[j])).astype(BF16) for j in heads]
        o = [_dot(qg[j], s16[j]) + _dot(att[j], v_new16[j]) for j in heads]
        for j in heads:
            state_ref[j] = s_old[j] * jnp.exp(g_last[j]) + _dot_tn(kd[j], v_new16[j])
        for j in heads:
            y = o[j] * lax.rsqrt(jnp.mean(o[j] * o[j], axis=-1, keepdims=True) + EPS) * ng
            o_ref[rs, cols[j]] = (y * _silu(z_ref[rs, cols[j]].astype(F32))).astype(o_ref.dtype)


def _gdn(proj, small, conv_w, alog_v, dtb_v, ng, *, tb=256, hg=8):
    b, s, _ = proj.shape
    nh = GDN_HEADS
    w = hg * GDN_D
    ng_blocks = nh // hg
    hq = COL_QKV // w
    hz = COL_Z // w
    return pl.pallas_call(
        functools.partial(_gdn_body, tb=tb, hg=hg),
        grid=(b, ng_blocks, s // tb),
        in_specs=[
            pl.BlockSpec((None, tb, w), lambda bi, g, t: (bi, t, hq + g)),
            pl.BlockSpec((None, tb, w), lambda bi, g, t: (bi, t, hq + ng_blocks + g)),
            pl.BlockSpec((None, tb, w), lambda bi, g, t: (bi, t, hq + 2 * ng_blocks + g)),
            pl.BlockSpec((None, tb, w), lambda bi, g, t: (bi, t, hz + g)),
            pl.BlockSpec((None, tb, LANES), lambda bi, g, t: (bi, t, 0)),
            pl.BlockSpec((GDN_CONV, w), lambda bi, g, t: (0, g)),
            pl.BlockSpec((GDN_CONV, w), lambda bi, g, t: (0, ng_blocks + g)),
            pl.BlockSpec((GDN_CONV, w), lambda bi, g, t: (0, 2 * ng_blocks + g)),
            pl.BlockSpec((1, LANES), lambda bi, g, t: (0, 0)),
            pl.BlockSpec((1, LANES), lambda bi, g, t: (0, 0)),
            pl.BlockSpec((1, LANES), lambda bi, g, t: (0, 0)),
        ],
        out_specs=pl.BlockSpec((None, tb, w), lambda bi, g, t: (bi, t, g)),
        out_shape=jax.ShapeDtypeStruct((b, s, nh * GDN_D), BF16),
        scratch_shapes=[
            pltpu.VMEM((hg, GDN_D, GDN_D), F32),
            pltpu.VMEM((3, SUBLANES, w), F32),
            pltpu.VMEM((SUBLANES + tb, w), F32),
        ],
        compiler_params=_cparams(("arbitrary", "arbitrary", "arbitrary")),
        name="gdn",
    )(proj, proj, proj, proj, small, conv_w, conv_w, conv_w, alog_v, dtb_v, ng)


DSA_QB = 128
DSA_KC = 512
DSA_ONES = 16
_KEY_NEG_INF = np.int32(np.array(0xFF800000, np.uint32).view(np.int32) ^ np.int32(0x7FFFFFFF))
_KEY_POS_INF_P1 = np.int32(0x7F800001)


def _key_to_f32(key):
    bits = key ^ ((key >> 31) & jnp.int32(0x7FFFFFFF))
    return lax.bitcast_convert_type(bits, F32)


def _tree_sum(xs):
    xs = list(xs)
    while len(xs) > 1:
        nxt = [xs[i] + xs[i + 1] for i in range(0, len(xs) - 1, 2)]
        if len(xs) % 2:
            nxt.append(xs[-1])
        xs = nxt
    return xs[0]


def _count_rows(hit):
    rows, n = hit.shape
    parts = hit.reshape(rows // SUBLANES, SUBLANES, n)
    return _tree_sum([parts[i] for i in range(rows // SUBLANES)])


def _dsa_body(q_ref, iq_ref, smq_ref, k_ref, v_ref, smk_ref, ikg_ref, o_ref,
              ikn_ref, vt_ref, s_ref, thrkey_ref, cnt_ref, jsel_ref, m_ref, alpha_ref, acc_ref, lg_ref, *, seq, k_top):
    qb = pl.program_id(1)
    kc = DSA_KC
    nq = DSA_QB
    rep = DSA_HEADS // DSA_KV_HEADS

    @pl.when(qb == 0)
    def _():
        def prep(i, carry):
            rs = pl.ds(pl.multiple_of(i * kc, kc), kc)
            sm = smk_ref[rs, :]
            lane = lax.broadcasted_iota(I32, sm.shape, 1)
            isk = lane < IDX_DIM
            mean = jnp.sum(jnp.where(isk, sm, 0.0), axis=-1, keepdims=True) * (1.0 / IDX_DIM)
            xc = jnp.where(isk, sm - mean, 0.0)
            var = jnp.sum(xc * xc, axis=-1, keepdims=True) * (1.0 / IDX_DIM)
            y = xc * lax.rsqrt(var + EPS) * ikg_ref[...]
            ikn_ref[0, rs, :] = y.astype(BF16)
            ikn_ref[1, rs, :] = pltpu.roll(y, IDX_DIM, axis=1).astype(BF16)
            vv = v_ref[rs, :].astype(F32)
            for g in range(DSA_KV_HEADS):
                vt_ref[g, i, 0:DSA_DH, :] = vv[:, g * DSA_DH:(g + 1) * DSA_DH].T.astype(BF16)
                vt_ref[g, i, DSA_DH:DSA_DH + DSA_ONES, :] = jnp.ones((DSA_ONES, kc), BF16)
            return carry
        lax.fori_loop(0, seq // kc, prep, 0)

    q_lo = qb * nq
    n_kc = (q_lo + nq + kc - 1) // kc
    qpos = q_lo + lax.broadcasted_iota(I32, (1, nq), 1)

    iq = iq_ref[...]
    n_pair = IDX_HEADS // 2
    iq_t = jnp.concatenate([iq[:, p * LANES:(p + 1) * LANES].astype(F32).T for p in range(n_pair)],
                           axis=1).astype(BF16)
    iw_t = smq_ref[...].T * ((IDX_HEADS ** -0.5) * (IDX_DIM ** -0.5))

    def score_chunk(i, carry):
        rs = pl.ds(pl.multiple_of(i * kc, kc), kc)
        lo = _dot(ikn_ref[0, rs, :], iq_t)
        hi = _dot(ikn_ref[1, rs, :], iq_t)
        sc = jnp.zeros((kc, nq), F32)
        for p in range(n_pair):
            cs = slice(p * nq, (p + 1) * nq)
            sc = sc + iw_t[SM_IW + 2 * p:SM_IW + 2 * p + 1, :] * jnp.maximum(lo[:, cs], 0.0)
            sc = sc + iw_t[SM_IW + 2 * p + 1:SM_IW + 2 * p + 2, :] * jnp.maximum(hi[:, cs], 0.0)
        kpos = i * kc + lax.broadcasted_iota(I32, (kc, nq), 0)
        s_ref[rs, :] = jnp.where(kpos <= qpos, sc + 0.0, -jnp.inf)
        return carry
    lax.fori_loop(0, n_kc, score_chunk, 0)

    kf = jnp.float32(k_top)

    def midpoint(lo, hi):
        return (lo >> 1) + (hi >> 1) + (lo & hi & 1)

    def bisect_static(rows):
        pieces = [slice(r0, min(r0 + kc, rows)) for r0 in range(0, rows, kc)]

        def count(hit_fn):
            parts = [_count_rows(jnp.where(hit_fn(s_ref[rs, :]), 1.0, 0.0)) for rs in pieces]
            return jnp.sum(_tree_sum(parts), axis=0, keepdims=True)

        def count_ge(thr):
            return count(lambda s: s >= thr)

        def bisect(_, carry):
            lo, hi = carry
            mid = midpoint(lo, hi)
            ok = count_ge(_key_to_f32(mid)) >= kf
            return jnp.where(ok, mid, lo), jnp.where(ok, hi, mid)
        lo0 = jnp.full((1, nq), _KEY_NEG_INF, I32)
        hi0 = jnp.full((1, nq), _KEY_POS_INF_P1, I32)
        lo, _ = lax.fori_loop(0, 32, bisect, (lo0, hi0))
        thrkey_ref[...] = lo
        thr = _key_to_f32(lo)
        cnt_ref[0] = count(lambda s: s > thr)
        cnt_ref[1] = count_ge(thr)

    for j in range(seq // nq):
        pl.when(qb == j)(functools.partial(bisect_static, (j + 1) * nq))
    thr = _key_to_f32(thrkey_ref[...])

    n_gt = cnt_ref[0]
    n_ge = cnt_ref[1]
    need = kf - n_gt
    jsel_ref[...] = jnp.full((1, nq), seq, I32)

    @pl.when(jnp.max(n_ge) > kf)
    def _():
        def count_eq_upto(jmax):
            def body(i, cnt):
                rs = pl.ds(pl.multiple_of(i * kc, kc), kc)
                kpos = i * kc + lax.broadcasted_iota(I32, (kc, nq), 0)
                hit = jnp.where(s_ref[rs, :] == thr, jnp.where(kpos <= jmax, 1.0, 0.0), 0.0)
                return cnt + _count_rows(hit)
            cnt = lax.fori_loop(0, n_kc, body, jnp.zeros((SUBLANES, nq), F32))
            return jnp.sum(cnt, axis=0, keepdims=True)

        def jbisect(_, carry):
            lo_j, hi_j = carry
            mid = (lo_j + hi_j) >> 1
            ok = count_eq_upto(mid) >= need
            return jnp.where(ok, lo_j, mid), jnp.where(ok, mid, hi_j)
        n_iter = int(np.ceil(np.log2(seq))) + 1
        lo_j0 = jnp.full((1, nq), -1, I32)
        hi_j0 = jnp.full((1, nq), seq - 1, I32)
        _, hi_j = lax.fori_loop(0, n_iter, jbisect, (lo_j0, hi_j0))
        jsel_ref[...] = hi_j

    jsel = jsel_ref[...]

    def selection_bias(i):
        s = s_ref[pl.ds(pl.multiple_of(i * kc, kc), kc), :]
        kpos = i * kc + lax.broadcasted_iota(I32, (kc, nq), 0)
        tie = jnp.where(s == thr, jnp.where(kpos <= jsel, 0.0, NEG_BIG), NEG_BIG)
        sel = jnp.where(s > thr, 0.0, tie)
        return jnp.where(kpos <= qpos, sel, NEG_BIG)

    qall = q_ref[...].astype(F32) * (DSA_DH ** -0.5 * np.log2(np.e))
    q_t = [jnp.concatenate([qall[:, (g * rep + r) * DSA_DH:(g * rep + r + 1) * DSA_DH].T
                            for r in range(rep)], axis=1).astype(BF16) for g in range(DSA_KV_HEADS)]
    m_ref[...] = jnp.full(m_ref.shape, NEG_BIG, F32)
    acc_ref[...] = jnp.zeros(acc_ref.shape, F32)

    def logits_phase(i):
        rs = pl.ds(pl.multiple_of(i * kc, kc), kc)
        bias_w = jnp.concatenate([selection_bias(i)] * rep, axis=1)
        for g in range(DSA_KV_HEADS):
            logit = _dot(k_ref[rs, g * DSA_DH:(g + 1) * DSA_DH], q_t[g]) + bias_w
            lg_ref[g] = logit
            m_old = m_ref[g]
            m_new = jnp.maximum(m_old, jnp.max(logit, axis=0, keepdims=True))
            alpha_ref[g] = jnp.exp2(m_old - m_new)
            m_ref[g] = m_new

    def value_phase(i):
        for g in range(DSA_KV_HEADS):
            p = jnp.exp2(lg_ref[g] - m_ref[g]).astype(BF16)
            acc_ref[g] = acc_ref[g] * alpha_ref[g] + _dot(vt_ref[g, i], p)

    logits_phase(0)

    def attn_chunk(i, carry):
        value_phase(i - 1)
        logits_phase(i)
        return carry
    lax.fori_loop(1, n_kc, attn_chunk, 0)
    value_phase(n_kc - 1)

    for g in range(DSA_KV_HEADS):
        acc = acc_ref[g]
        ot = acc[0:DSA_DH] / acc[DSA_DH:DSA_DH + 1]
        for r in range(rep):
            hh = g * rep + r
            o_ref[:, hh * DSA_DH:(hh + 1) * DSA_DH] = ot[:, r * nq:(r + 1) * nq].T.astype(o_ref.dtype)


def _dsa(proj, small, ikg):
    b, s, _ = proj.shape
    nq = DSA_QB
    k_top = min(TOPK_MAX, s // 4)
    rep = DSA_HEADS // DSA_KV_HEADS
    return pl.pallas_call(
        functools.partial(_dsa_body, seq=s, k_top=k_top),
        grid=(b, s // nq),
        in_specs=[
            pl.BlockSpec((None, nq, DSA_HEADS * DSA_DH), lambda bi, qb: (bi, qb, COL_DQ // (DSA_HEADS * DSA_DH))),
            pl.BlockSpec((None, nq, IDX_HEADS * IDX_DIM), lambda bi, qb: (bi, qb, COL_IQ // (IDX_HEADS * IDX_DIM))),
            pl.BlockSpec((None, nq, LANES), lambda bi, qb: (bi, qb, 0)),
            pl.BlockSpec((None, s, DSA_KV_HEADS * DSA_DH), lambda bi, qb: (bi, 0, COL_DK // (DSA_KV_HEADS * DSA_DH))),
            pl.BlockSpec((None, s, DSA_KV_HEADS * DSA_DH), lambda bi, qb: (bi, 0, COL_DV // (DSA_KV_HEADS * DSA_DH))),
            pl.BlockSpec((None, s, LANES), lambda bi, qb: (bi, 0, 0)),
            pl.BlockSpec((1, LANES), lambda bi, qb: (0, 0)),
        ],
        out_specs=pl.BlockSpec((None, nq, DSA_HEADS * DSA_DH), lambda bi, qb: (bi, qb, 0)),
        out_shape=jax.ShapeDtypeStruct((b, s, DSA_HEADS * DSA_DH), BF16),
        scratch_shapes=[
            pltpu.VMEM((2, s, LANES), BF16),
            pltpu.VMEM((DSA_KV_HEADS, s // DSA_KC, DSA_DH + DSA_ONES, DSA_KC), BF16),
            pltpu.VMEM((s, nq), F32),
            pltpu.VMEM((1, nq), I32),
            pltpu.VMEM((2, 1, nq), F32),
            pltpu.VMEM((1, nq), I32),
            pltpu.VMEM((DSA_KV_HEADS, 1, rep * nq), F32),
            pltpu.VMEM((DSA_KV_HEADS, 1, rep * nq), F32),
            pltpu.VMEM((DSA_KV_HEADS, DSA_DH + DSA_ONES, rep * nq), F32),
            pltpu.VMEM((DSA_KV_HEADS, DSA_KC, rep * nq), F32),
        ],
        compiler_params=_cparams(("arbitrary", "arbitrary")),
        name="dsa",
    )(proj, proj, small, proj, proj, small, ikg)


FFN_CHUNK = 256


def _mix_ffn_body(x_ref, ga_ref, gb_ref, ba_ref, bb_ref, og_ref, od_ref, wo_ref,
                  g2_ref, wup_ref, cw_ref, cb_ref, wdn_ref, gf_ref, o_ref,
                  carry_ref, xx_ref, perm_ref, *, tm):
    t = pl.program_id(1)

    @pl.when(t == 0)
    def _():
        carry_ref[...] = jnp.zeros_like(carry_ref)

    sa = _sigmoid(ga_ref[...].astype(F32) + ba_ref[...])
    sb = _sigmoid(gb_ref[...].astype(F32) + bb_ref[...])
    mixed = sa * og_ref[...].astype(F32) + sb * od_ref[...].astype(F32)
    h = x_ref[...] + _dot(mixed.astype(BF16), wo_ref[...])
    hn =((h * lax.rsqrt(jnp.mean(h * h, axis=-1, keepdims=True) + EPS)) * g2_ref[...]).astype(BF16)

    def up_cols(c0):
        cs = slice(c0, c0 + FFN_CHUNK)
        up = _dot(hn, wup_ref[:, cs])
        for k in range(FFN_CHUNK // LANES):
            blk = c0 // LANES + k
            ls = slice(k * LANES, (k + 1) * LANES)
            xx_ref[blk, 0:SUBLANES, :] = carry_ref[:, c0 + k * LANES:c0 + (k + 1) * LANES]
            xx_ref[blk, SUBLANES:SUBLANES + tm, :] = up[:, ls]
        carry_ref[:, cs] = up[tm - SUBLANES:tm, :]

    def conv_cols(c0):
        cols = []
        for k in range(FFN_CHUNK // LANES):
            blk = c0 // LANES + k
            ls = slice(c0 + k * LANES, c0 + (k + 1) * LANES)
            w = cw_ref[:, ls]
            groups = []
            for r in range(SUBLANES):
                y = cb_ref[:, ls]
                for j in range(FFN_CONV):
                    s0 = SUBLANES - (FFN_CONV - 1) + j + r
                    y = y + w[j:j + 1, :] * xx_ref[blk, pl.ds(s0, tm // SUBLANES, stride=SUBLANES), :]
                groups.append(y)
            cols.append(jnp.concatenate(groups, axis=0))
        return jnp.concatenate(cols, axis=1)

    n_chunk = D_FF // FFN_CHUNK
    up_cols(0)
    up_cols(D_FF)
    acc = jnp.zeros((tm, D_MODEL), F32)
    for c in range(n_chunk):
        if c + 1 < n_chunk:
            up_cols((c + 1) * FFN_CHUNK)
            up_cols(D_FF + (c + 1) * FFN_CHUNK)
        gate = conv_cols(c * FFN_CHUNK)
        val = conv_cols(D_FF + c * FFN_CHUNK)
        act = (_silu(gate) * val).astype(BF16)
        acc = acc + _dot(act, wdn_ref[c * FFN_CHUNK:(c + 1) * FFN_CHUNK, :])
    g_rows = tm // SUBLANES
    for k in range(D_MODEL // LANES):
        for r in range(SUBLANES):
            perm_ref[k, pl.ds(r, g_rows, stride=SUBLANES), :] = acc[r * g_rows:(r + 1) * g_rows,
                                                                    k * LANES:(k + 1) * LANES]
    h2 = h + jnp.concatenate([perm_ref[k] for k in range(D_MODEL // LANES)], axis=1)
    o_ref[...] = (h2 * lax.rsqrt(jnp.mean(h2 * h2, axis=-1, keepdims=True) + EPS)) * gf_ref[...]


def _mix_ffn(x, proj, bias_a, bias_b, o_gdn, o_dsa, w_out, g2, w_up, conv_w, conv_b, w_down, gf, *, tm=256):
    b, s, _ = x.shape
    const = lambda bi, t: (0, 0)
    row = lambda bi, t: (bi, t, 0)
    return pl.pallas_call(
        functools.partial(_mix_ffn_body, tm=tm),
        grid=(b, s // tm),
        in_specs=[
            pl.BlockSpec((None, tm, D_MODEL), row),
            pl.BlockSpec((None, tm, D_MODEL), lambda bi, t: (bi, t, COL_GA // D_MODEL)),
            pl.BlockSpec((None, tm, D_MODEL), lambda bi, t: (bi, t, COL_GB // D_MODEL)),
            pl.BlockSpec((1, D_MODEL), const),
            pl.BlockSpec((1, D_MODEL), const),
            pl.BlockSpec((None, tm, D_MODEL), row),
            pl.BlockSpec((None, tm, D_MODEL), row),
            pl.BlockSpec((D_MODEL, D_MODEL), const, pipeline_mode=pl.Buffered(1)),
            pl.BlockSpec((1, D_MODEL), const),
            pl.BlockSpec((D_MODEL, 2 * D_FF), const, pipeline_mode=pl.Buffered(1)),
            pl.BlockSpec((FFN_CONV, 2 * D_FF), const),
            pl.BlockSpec((1, 2 * D_FF), const),
            pl.BlockSpec((D_FF, D_MODEL), const, pipeline_mode=pl.Buffered(1)),
            pl.BlockSpec((1, D_MODEL), const),
        ],
        out_specs=pl.BlockSpec((None, tm, D_MODEL), lambda bi, t: (bi, t, 0)),
        out_shape=jax.ShapeDtypeStruct((b, s, D_MODEL), F32),
        scratch_shapes=[
            pltpu.VMEM((SUBLANES, 2 * D_FF), F32),
            pltpu.VMEM((2 * D_FF // LANES, SUBLANES + tm, LANES), F32),
            pltpu.VMEM((D_MODEL // LANES, tm, LANES), F32),
        ],
        compiler_params=_cparams(("arbitrary", "arbitrary")),
        name="mix_ffn",
    )(x, proj, proj, bias_a, bias_b, o_gdn, o_dsa, w_out, g2, w_up, conv_w, conv_b, w_down, gf)


def _rearranged_w_in(w):
    o = np.cumsum([0, 3072, 1024, 8, 8, 1024, 256, 256, 512, 64, 8, 1024, 1024])
    g_qkv, g_z, g_a, g_b, d_q, d_k, d_v, i_q, i_k, i_w, gate_a, gate_b = (
        w[:, int(o[i]):int(o[i + 1])] for i in range(12))
    big = jnp.concatenate([g_qkv, g_z, d_q, d_k, d_v, i_q, gate_a, gate_b], axis=1).astype(BF16)
    pad = jnp.zeros((w.shape[0], LANES - (IDX_DIM + 3 * 8)), w.dtype)
    small = jnp.concatenate([i_k, g_a, g_b, i_w, pad], axis=1).astype(BF16)
    return big, small


def _lane_vec(v, offset):
    return jnp.zeros((1, LANES), F32).at[0, offset:offset + v.shape[0]].set(v.astype(F32))


def kernel(x, norm1_g, w_in, gdn_conv_w, gdn_a_log, gdn_dt_bias, gdn_norm_g, idx_k_norm_g,
           branch_gate_b, w_out, norm2_g, ffn_w_up, ffn_conv_w, ffn_conv_b, ffn_w_down, final_g):
    b, s, d = x.shape
    depth = norm1_g.shape[0]
    assert depth == 1, "the final RMSNorm is fused into the (single) channel-mixer call"
    h = x
    for l in range(depth):
        w_big, w_small = _rearranged_w_in(w_in[l])
        proj2, small2 = _inproj(h.reshape(b * s, d), norm1_g[l][None, :], w_big, w_small)
        proj = proj2.reshape(b, s, N_BIG)
        small = small2.reshape(b, s, LANES)
        o_gdn = _gdn(proj, small, gdn_conv_w[l], _lane_vec(gdn_a_log[l], SM_A),
                     _lane_vec(gdn_dt_bias[l], SM_A), gdn_norm_g[l][None, :].astype(F32))
        o_dsa = _dsa(proj, small, _lane_vec(idx_k_norm_g[l], SM_IK))
        gb = branch_gate_b[l]
        h = _mix_ffn(h, proj, gb[None, :D_MODEL], gb[None, D_MODEL:], o_gdn, o_dsa, w_out[l].astype(BF16),
                     norm2_g[l][None, :], ffn_w_up[l].astype(BF16), ffn_conv_w[l],
                     ffn_conv_b[l][None, :], ffn_w_down[l].astype(BF16), final_g[None, :])
    return h
```

```python
import functools

import jax
import jax.numpy as jnp
import numpy as np
from jax import lax
from jax.experimental import pallas as pl
from jax.experimental.pallas import tpu as pltpu

F32 = jnp.float32
BF16 = jnp.bfloat16
I32 = jnp.int32

LANES = 128
SUBLANES = 8
VMEM_LIMIT = 56 * 1024 * 1024

D_MODEL = 1024
GDN_HEADS = 8
GDN_D = 128
GDN_CONV = 4
DSA_HEADS = 8
DSA_KV_HEADS = 2
DSA_DH = 128
IDX_HEADS = 8
IDX_DIM = 64
TOPK_MAX = 256
D_FF = 2816
FFN_CONV = 3
EPS = 1e-6

COL_QKV = 0
COL_Z = 3072
COL_DQ = 4096
COL_DK = 5120
COL_DV = 5376
COL_IQ = 5632
COL_GA = 6144
COL_GB = 7168
N_BIG = 8192
SM_IK = 0
SM_A = 64
SM_B = 72
SM_IW = 80

NEG_BIG = -1e30


def _cparams(sem):
    return pltpu.CompilerParams(dimension_semantics=sem, vmem_limit_bytes=VMEM_LIMIT)


def _dot(a, b):
    return jnp.dot(a, b, preferred_element_type=F32)


def _dot_nt(a, b):
    return lax.dot_general(a, b, (((1,), (1,)), ((), ())), preferred_element_type=F32)


def _dot_tn(a, b):
    return lax.dot_general(a, b, (((0,), (0,)), ((), ())), preferred_element_type=F32)


def _split3(x):
    h = x.astype(BF16)
    r = x - h.astype(F32)
    m = r.astype(BF16)
    l = (r - m.astype(F32)).astype(BF16)
    return h, m, l


def _sigmoid(x):
    return 1.0 / (1.0 + jnp.exp(-x))


def _silu(x):
    h = 0.5 * x
    return h + h * jnp.tanh(h)


def _softplus(x):
    return jnp.maximum(x, 0.0) + jnp.log(1.0 + jnp.exp(-jnp.abs(x)))


OUT_WIDTHS = (COL_DQ - COL_QKV,
              COL_DK - COL_DQ,
              COL_DV - COL_DK,
              COL_IQ - COL_DV,
              COL_GA - COL_IQ,
              N_BIG - COL_GA)


def _inproj_body(x_ref, g_ref, w_ref, ws_ref, *out_refs, col_chunk):
    small_ref = out_refs[-1]
    x = x_ref[...]
    ms = jnp.mean(x * x, axis=-1, keepdims=True)
    u = ((x * lax.rsqrt(ms + EPS)) * g_ref[...]).astype(BF16)
    starts = np.cumsum((0,) + OUT_WIDTHS)
    for c in range(N_BIG // col_chunk):
        c0 = c * col_chunk
        res = _dot(u, w_ref[:, c0:c0 + col_chunk]).astype(BF16)
        for ref, s0, width in zip(out_refs, starts, OUT_WIDTHS):
            lo, hi = max(c0, int(s0)), min(c0 + col_chunk, int(s0) + width)
            if lo < hi:
                ref[:, lo - int(s0):hi - int(s0)] = res[:, lo - c0:hi - c0]
    small_ref[...] = _dot(u, ws_ref[...])


def _inproj(x2, g, w_big, w_small, *, tm=256):
    n = x2.shape[0]
    const = lambda i: (0, 0)
    widths = OUT_WIDTHS + (LANES,)
    return pl.pallas_call(
        functools.partial(_inproj_body, col_chunk=1024),
        grid=(n // tm,),
        in_specs=[
            pl.BlockSpec((tm, D_MODEL), lambda i: (i, 0)),
            pl.BlockSpec((1, D_MODEL), const),
            pl.BlockSpec((D_MODEL, N_BIG), const, pipeline_mode=pl.Buffered(1)),
            pl.BlockSpec((D_MODEL, LANES), const, pipeline_mode=pl.Buffered(1)),
        ],
        out_specs=[pl.BlockSpec((tm, wd), lambda i: (i, 0)) for wd in widths],
        out_shape=[jax.ShapeDtypeStruct((n, wd), BF16) for wd in OUT_WIDTHS]
        + [jax.ShapeDtypeStruct((n, LANES), F32)],
        compiler_params=_cparams(("arbitrary",)),
        name="inproj",
    )(x2, g, w_big, w_small)


GDN_C = 128


def _tri_inverse(a_list):
    n = a_list[0].shape[0]
    row = lax.broadcasted_iota(I32, (n, n), 0)
    col = lax.broadcasted_iota(I32, (n, n), 1)
    eye = jnp.where(row == col, 1.0, 0.0).astype(F32)

    def same_block(size):
        sh = size.bit_length() - 1
        return (row >> sh) == (col >> sh)

    def square(ps):
        return [_dot(p, p).astype(BF16) for p in ps]

    def grow(ts, ps):
        return [t + _dot(t.astype(BF16), p) for t, p in zip(ts, ps)]

    p1 = [jnp.where(same_block(16), a, 0.0) for a in a_list]
    p2s = square([p.astype(BF16) for p in p1])
    ts = [eye - p for p in p1]
    p4s = square(p2s)
    ts = grow(ts, p2s)
    p8s = square(p4s)
    ts = grow(ts, p4s)
    ts = grow(ts, p8s)
    size = 16
    while size < n:
        in_pair = same_block(2 * size) & jnp.logical_not(same_block(size))
        offs = [jnp.where(in_pair, a, 0.0).astype(BF16) for a in a_list]
        tss = [t.astype(BF16) for t in ts]
        mids = [_dot(t2, o).astype(BF16) for t2, o in zip(tss, offs)]
        ts = [t - _dot(m, t2) for t, m, t2 in zip(ts, mids, tss)]
        size *= 2
    return ts


def _gdn_body(qp_ref, kp_ref, vp_ref, z_ref, sm_ref, wq_ref, wk_ref, wv_ref,
              alog_ref, dtb_ref, ng_ref, o_ref,
              state_ref, halo_ref, xx_ref, *, tb, hg):
    hgi = pl.program_id(1)
    t = pl.program_id(2)

    @pl.when(t == 0)
    def _():
        state_ref[...] = jnp.zeros_like(state_ref)
        halo_ref[...] = jnp.zeros_like(halo_ref)

    def conv_silu(idx, x_ref, w_ref):
        x = x_ref[...].astype(F32)
        xx_ref[0:SUBLANES, :] = halo_ref[idx]
        xx_ref[SUBLANES:SUBLANES + tb, :] = x
        halo_ref[idx] = x[tb - SUBLANES:tb, :]
        w = w_ref[...]
        y = w[GDN_CONV - 1:GDN_CONV, :] * x
        for j in range(GDN_CONV - 1):
            s0 = SUBLANES - (GDN_CONV - 1) + j
            y = y + w[j:j + 1, :] * xx_ref[s0:s0 + tb, :]
        return _silu(y)

    q_all = conv_silu(0, qp_ref, wq_ref)
    k_all = conv_silu(1, kp_ref, wk_ref)
    v_all = conv_silu(2, vp_ref, wv_ref)

    sm = sm_ref[...]
    ld_all = -jnp.exp(alog_ref[...]) * _softplus(sm + dtb_ref[...])
    beta_all = _sigmoid(sm)

    c = GDN_C
    row = lax.broadcasted_iota(I32, (c, c), 0)
    col = lax.broadcasted_iota(I32, (c, c), 1)
    lane = lax.broadcasted_iota(I32, (c, LANES), 1)
    tri_incl = jnp.where(row >= col, 1.0, 0.0).astype(BF16)
    ng = ng_ref[...]

    for ci in range(tb // c):
        rs = slice(ci * c, (ci + 1) * c)
        l1, l2, l3 = _split3(ld_all[rs])
        g_all = _dot(tri_incl, l1) + (_dot(tri_incl, l2) + _dot(tri_incl, l3))
        heads = range(hg)
        cols = [slice(j * GDN_D, (j + 1) * GDN_D) for j in heads]
        gc, bc, gamma, eg, g_last, qc, kc, kb, kc16 = ([None] * hg for _ in range(9))
        for j in heads:
            h = hgi * hg + j
            gcol = jnp.sum(jnp.where(lane == SM_A + h, g_all, 0.0), axis=-1, keepdims=True)
            bc[j] = jnp.sum(jnp.where(lane == SM_B + h, beta_all[rs], 0.0), axis=-1, keepdims=True)
            gc[j] = jnp.broadcast_to(gcol, (c, c))
            gamma[j] = jnp.exp(jnp.where(row >= col, gc[j] - gc[j].T, -jnp.inf))
            eg[j] = jnp.exp(gc[j])
            g_last[j] = gc[j][c - 1:c, :]
            q_h = q_all[rs, cols[j]]
            k_h = k_all[rs, cols[j]]
            qc[j] = q_h * (lax.rsqrt(jnp.sum(q_h * q_h, axis=-1, keepdims=True) + EPS) * (GDN_D ** -0.5))
            kc[j] = k_h * lax.rsqrt(jnp.sum(k_h * k_h, axis=-1, keepdims=True) + EPS)
            kb[j] = kc[j] * bc[j]
            kc16[j] = kc[j].astype(BF16)
        a = [jnp.where(row > col, _dot_nt(kb[j].astype(BF16), kc16[j]) * gamma[j], 0.0) for j in heads]
        att = [(_dot_nt(qc[j].astype(BF16), kc16[j]) * gamma[j]).astype(BF16) for j in heads]
        tinv = [t.astype(BF16) for t in _tri_inverse(a)]
        u = [_dot(tinv[j], (v_all[rs, cols[j]] * bc[j]).astype(BF16)) for j in heads]
        w = [_dot(tinv[j], (kb[j] * eg[j]).astype(BF16)).astype(BF16) for j in heads]
        qg = [(qc[j] * eg[j]).astype(BF16) for j in heads]
        kd = [(kc[j] * jnp.exp(g_last[j] - gc[j])).astype(BF16) for j in heads]

        s_old = [state_ref[j] for j in heads]
        s16 = [s.astype(BF16) for s in s_old]
        v_new16 = [(u[j] - _dot(w[j], s16[j])).astype(BF16) for j in heads]
        o = [_dot(qg[j], s16[j]) + _dot(att[j], v_new16[j]) for j in heads]
        for j in heads:
            state_ref[j] = s_old[j] * jnp.exp(g_last[j]) + _dot_tn(kd[j], v_new16[j])
        for j in heads:
            y = o[j] * lax.rsqrt(jnp.mean(o[j] * o[j], axis=-1, keepdims=True) + EPS) * ng
            o_ref[rs, cols[j]] = (y * _silu(z_ref[rs, cols[j]].astype(F32))).astype(o_ref.dtype)


def _gdn(proj, small, conv_w, alog_v, dtb_v, ng, *, tb=256, hg=8):
    b, s, _ = proj.shape
    nh = GDN_HEADS
    w = hg * GDN_D
    ng_blocks = nh // hg
    hq = COL_QKV // w
    hz = COL_Z // w
    return pl.pallas_call(
        functools.partial(_gdn_body, tb=tb, hg=hg),
        grid=(b, ng_blocks, s // tb),
        in_specs=[
            pl.BlockSpec((None, tb, w), lambda bi, g, t: (bi, t, hq + g)),
            pl.BlockSpec((None, tb, w), lambda bi, g, t: (bi, t, hq + ng_blocks + g)),
            pl.BlockSpec((None, tb, w), lambda bi, g, t: (bi, t, hq + 2 * ng_blocks + g)),
            pl.BlockSpec((None, tb, w), lambda bi, g, t: (bi, t, hz + g)),
            pl.BlockSpec((None, tb, LANES), lambda bi, g, t: (bi, t, 0)),
            pl.BlockSpec((GDN_CONV, w), lambda bi, g, t: (0, g)),
            pl.BlockSpec((GDN_CONV, w), lambda bi, g, t: (0, ng_blocks + g)),
            pl.BlockSpec((GDN_CONV, w), lambda bi, g, t: (0, 2 * ng_blocks + g)),
            pl.BlockSpec((1, LANES), lambda bi, g, t: (0, 0)),
            pl.BlockSpec((1, LANES), lambda bi, g, t: (0, 0)),
            pl.BlockSpec((1, LANES), lambda bi, g, t: (0, 0)),
        ],
        out_specs=pl.BlockSpec((None, tb, w), lambda bi, g, t: (bi, t, g)),
        out_shape=jax.ShapeDtypeStruct((b, s, nh * GDN_D), BF16),
        scratch_shapes=[
            pltpu.VMEM((hg, GDN_D, GDN_D), F32),
            pltpu.VMEM((3, SUBLANES, w), F32),
            pltpu.VMEM((SUBLANES + tb, w), F32),
        ],
        compiler_params=_cparams(("arbitrary", "arbitrary", "arbitrary")),
        name="gdn",
    )(proj, proj, proj, proj, small, conv_w, conv_w, conv_w, alog_v, dtb_v, ng)


DSA_QB = 128
DSA_KC = 512
DSA_ONES = 16
_KEY_NEG_INF = np.int32(np.array(0xFF800000, np.uint32).view(np.int32) ^ np.int32(0x7FFFFFFF))
_KEY_POS_INF_P1 = np.int32(0x7F800001)


def _key_to_f32(key):
    bits = key ^ ((key >> 31) & jnp.int32(0x7FFFFFFF))
    return lax.bitcast_convert_type(bits, F32)


def _tree_sum(xs):
    xs = list(xs)
    while len(xs) > 1:
        nxt = [xs[i] + xs[i + 1] for i in range(0, len(xs) - 1, 2)]
        if len(xs) % 2:
            nxt.append(xs[-1])
        xs = nxt
    return xs[0]


def _count_rows(hit):
    rows, n = hit.shape
    parts = hit.reshape(rows // SUBLANES, SUBLANES, n)
    return _tree_sum([parts[i] for i in range(rows // SUBLANES)])


def _dsa_body(q_ref, iq_ref, smq_ref, k_ref, v_ref, smk_ref, ikg_ref, o_ref,
              ikn_ref, vt_ref, s_ref, thrkey_ref, cnt_ref, jsel_ref, m_ref, alpha_ref, acc_ref, lg_ref, *, seq, k_top):
    qb = pl.program_id(1)
    kc = DSA_KC
    nq = DSA_QB
    rep = DSA_HEADS // DSA_KV_HEADS

    @pl.when(qb == 0)
    def _():
        def prep(i, carry):
            rs = pl.ds(pl.multiple_of(i * kc, kc), kc)
            sm = smk_ref[rs, :]
            lane = lax.broadcasted_iota(I32, sm.shape, 1)
            isk = lane < IDX_DIM
            mean = jnp.sum(jnp.where(isk, sm, 0.0), axis=-1, keepdims=True) * (1.0 / IDX_DIM)
            xc = jnp.where(isk, sm - mean, 0.0)
            var = jnp.sum(xc * xc, axis=-1, keepdims=True) * (1.0 / IDX_DIM)
            y = xc * lax.rsqrt(var + EPS) * ikg_ref[...]
            ikn_ref[0, rs, :] = y.astype(BF16)
            ikn_ref[1, rs, :] = pltpu.roll(y, IDX_DIM, axis=1).astype(BF16)
            vv = v_ref[rs, :].astype(F32)
            for g in range(DSA_KV_HEADS):
                vt_ref[g, i, 0:DSA_DH, :] = vv[:, g * DSA_DH:(g + 1) * DSA_DH].T.astype(BF16)
                vt_ref[g, i, DSA_DH:DSA_DH + DSA_ONES, :] = jnp.ones((DSA_ONES, kc), BF16)
            return carry
        lax.fori_loop(0, seq // kc, prep, 0)

    q_lo = qb * nq
    n_kc = (q_lo + nq + kc - 1) // kc
    qpos = q_lo + lax.broadcasted_iota(I32, (1, nq), 1)

    iq = iq_ref[...]
    n_pair = IDX_HEADS // 2
    iq_t = jnp.concatenate([iq[:, p * LANES:(p + 1) * LANES].astype(F32).T for p in range(n_pair)],
                           axis=1).astype(BF16)
    iw_t = smq_ref[...].T * ((IDX_HEADS ** -0.5) * (IDX_DIM ** -0.5))

    def score_chunk(i, carry):
        rs = pl.ds(pl.multiple_of(i * kc, kc), kc)
        lo = _dot(ikn_ref[0, rs, :], iq_t)
        hi = _dot(ikn_ref[1, rs, :], iq_t)
        sc = jnp.zeros((kc, nq), F32)
        for p in range(n_pair):
            cs = slice(p * nq, (p + 1) * nq)
            sc = sc + iw_t[SM_IW + 2 * p:SM_IW + 2 * p + 1, :] * jnp.maximum(lo[:, cs], 0.0)
            sc = sc + iw_t[SM_IW + 2 * p + 1:SM_IW + 2 * p + 2, :] * jnp.maximum(hi[:, cs], 0.0)
        kpos = i * kc + lax.broadcasted_iota(I32, (kc, nq), 0)
        s_ref[rs, :] = jnp.where(kpos <= qpos, sc + 0.0, -jnp.inf)
        return carry
    lax.fori_loop(0, n_kc, score_chunk, 0)

    kf = jnp.float32(k_top)

    def midpoint(lo, hi):
        return (lo >> 1) + (hi >> 1) + (lo & hi & 1)

    def bisect_static(rows):
        pieces = [slice(r0, min(r0 + kc, rows)) for r0 in range(0, rows, kc)]

        def count(hit_fn):
            parts = [_count_rows(jnp.where(hit_fn(s_ref[rs, :]), 1.0, 0.0)) for rs in pieces]
            return jnp.sum(_tree_sum(parts), axis=0, keepdims=True)

        def count_ge(thr):
            return count(lambda s: s >= thr)

        def bisect(_, carry):
            lo, hi = carry
            mid = midpoint(lo, hi)
            ok = count_ge(_key_to_f32(mid)) >= kf
            return jnp.where(ok, mid, lo), jnp.where(ok, hi, mid)
        lo0 = jnp.full((1, nq), _KEY_NEG_INF, I32)
        hi0 = jnp.full((1, nq), _KEY_POS_INF_P1, I32)
        lo, _ = lax.fori_loop(0, 32, bisect, (lo0, hi0))
        thrkey_ref[...] = lo
        thr = _key_to_f32(lo)
        cnt_ref[0] = count(lambda s: s > thr)
        cnt_ref[1] = count_ge(thr)

    for j in range(seq // nq):
        pl.when(qb == j)(functools.partial(bisect_static, (j + 1) * nq))
    thr = _key_to_f32(thrkey_ref[...])

    n_gt = cnt_ref[0]
    n_ge = cnt_ref[1]
    need = kf - n_gt
    jsel_ref[...] = jnp.full((1, nq), seq, I32)

    @pl.when(jnp.max(n_ge) > kf)
    def _():
        def count_eq_upto(jmax):
            def body(i, cnt):
                rs = pl.ds(pl.multiple_of(i * kc, kc), kc)
                kpos = i * kc + lax.broadcasted_iota(I32, (kc, nq), 0)
                hit = jnp.where(s_ref[rs, :] == thr, jnp.where(kpos <= jmax, 1.0, 0.0), 0.0)
                return cnt + _count_rows(hit)
            cnt = lax.fori_loop(0, n_kc, body, jnp.zeros((SUBLANES, nq), F32))
            return jnp.sum(cnt, axis=0, keepdims=True)

        def jbisect(_, carry):
            lo_j, hi_j = carry
            mid = (lo_j + hi_j) >> 1
            ok = count_eq_upto(mid) >= need
            return jnp.where(ok, lo_j, mid), jnp.where(ok, mid, hi_j)
        n_iter = int(np.ceil(np.log2(seq))) + 1
        lo_j0 = jnp.full((1, nq), -1, I32)
        hi_j0 = jnp.full((1, nq), seq - 1, I32)
        _, hi_j = lax.fori_loop(0, n_iter, jbisect, (lo_j0, hi_j0))
        jsel_ref[...] = hi_j

    jsel = jsel_ref[...]

    def selection_bias(i):
        s = s_ref[pl.ds(pl.multiple_of(i * kc, kc), kc), :]
        kpos = i * kc + lax.broadcasted_iota(I32, (kc, nq), 0)
        tie = jnp.where(s == thr, jnp.where(kpos <= jsel, 0.0, NEG_BIG), NEG_BIG)
        sel = jnp.where(s > thr, 0.0, tie)
        return jnp.where(kpos <= qpos, sel, NEG_BIG)

    qall = q_ref[...].astype(F32) * (DSA_DH ** -0.5 * np.log2(np.e))
    q_t = [jnp.concatenate([qall[:, (g * rep + r) * DSA_DH:(g * rep + r + 1) * DSA_DH].T
                            for r in range(rep)], axis=1).astype(BF16) for g in range(DSA_KV_HEADS)]
    m_ref[...] = jnp.full(m_ref.shape, NEG_BIG, F32)
    acc_ref[...] = jnp.zeros(acc_ref.shape, F32)

    def logits_phase(i):
        rs = pl.ds(pl.multiple_of(i * kc, kc), kc)
        bias_w = jnp.concatenate([selection_bias(i)] * rep, axis=1)
        for g in range(DSA_KV_HEADS):
            logit = _dot(k_ref[rs, g * DSA_DH:(g + 1) * DSA_DH], q_t[g]) + bias_w
            lg_ref[g] = logit
            m_old = m_ref[g]
            m_new = jnp.maximum(m_old, jnp.max(logit, axis=0, keepdims=True))
            alpha_ref[g] = jnp.exp2(m_old - m_new)
            m_ref[g] = m_new

    def value_phase(i):
        for g in range(DSA_KV_HEADS):
            p = jnp.exp2(lg_ref[g] - m_ref[g]).astype(BF16)
            acc_ref[g] = acc_ref[g] * alpha_ref[g] + _dot(vt_ref[g, i], p)

    logits_phase(0)

    def attn_chunk(i, carry):
        value_phase(i - 1)
        logits_phase(i)
        return carry
    lax.fori_loop(1, n_kc, attn_chunk, 0)
    value_phase(n_kc - 1)

    for g in range(DSA_KV_HEADS):
        acc = acc_ref[g]
        ot = acc[0:DSA_DH] / acc[DSA_DH:DSA_DH + 1]
        for r in range(rep):
            hh = g * rep + r
            o_ref[:, hh * DSA_DH:(hh + 1) * DSA_DH] = ot[:, r * nq:(r + 1) * nq].T.astype(o_ref.dtype)


def _dsa(dq, iq, dk, dv, small, ikg):
    b, s, _ = dq.shape
    nq = DSA_QB
    k_top = min(TOPK_MAX, s // 4)
    rep = DSA_HEADS // DSA_KV_HEADS
    kvw = DSA_KV_HEADS * DSA_DH
    return pl.pallas_call(
        functools.partial(_dsa_body, seq=s, k_top=k_top),
        grid=(b, s // nq),
        in_specs=[
            pl.BlockSpec((None, nq, DSA_HEADS * DSA_DH), lambda bi, qb: (bi, qb, 0)),
            pl.BlockSpec((None, nq, IDX_HEADS * IDX_DIM), lambda bi, qb: (bi, qb, 0)),
            pl.BlockSpec((None, nq, LANES), lambda bi, qb: (bi, qb, 0)),
            pl.BlockSpec((None, s, kvw), lambda bi, qb: (bi, 0, 0)),
            pl.BlockSpec((None, s, kvw), lambda bi, qb: (bi, 0, 0)),
            pl.BlockSpec((None, s, LANES), lambda bi, qb: (bi, 0, 0)),
            pl.BlockSpec((1, LANES), lambda bi, qb: (0, 0)),
        ],
        out_specs=pl.BlockSpec((None, nq, DSA_HEADS * DSA_DH), lambda bi, qb: (bi, qb, 0)),
        out_shape=jax.ShapeDtypeStruct((b, s, DSA_HEADS * DSA_DH), BF16),
        scratch_shapes=[
            pltpu.VMEM((2, s, LANES), BF16),
            pltpu.VMEM((DSA_KV_HEADS, s // DSA_KC, DSA_DH + DSA_ONES, DSA_KC), BF16),
            pltpu.VMEM((s, nq), F32),
            pltpu.VMEM((1, nq), I32),
            pltpu.VMEM((2, 1, nq), F32),
            pltpu.VMEM((1, nq), I32),
            pltpu.VMEM((DSA_KV_HEADS, 1, rep * nq), F32),
            pltpu.VMEM((DSA_KV_HEADS, 1, rep * nq), F32),
            pltpu.VMEM((DSA_KV_HEADS, DSA_DH + DSA_ONES, rep * nq), F32),
            pltpu.VMEM((DSA_KV_HEADS, DSA_KC, rep * nq), F32),
        ],
        compiler_params=_cparams(("arbitrary", "arbitrary")),
        name="dsa",
    )(dq, iq, small, dk, dv, small, ikg)


FFN_CHUNK = 256


def _mix_ffn_body(x_ref, ga_ref, gb_ref, ba_ref, bb_ref, og_ref, od_ref, wo_ref,
                  g2_ref, wup_ref, cw_ref, cb_ref, wdn_ref, gf_ref, o_ref,
                  carry_ref, xx_ref, perm_ref, *, tm):
    t = pl.program_id(1)

    @pl.when(t == 0)
    def _():
        carry_ref[...] = jnp.zeros_like(carry_ref)

    sa = _sigmoid(ga_ref[...].astype(F32) + ba_ref[...])
    sb = _sigmoid(gb_ref[...].astype(F32) + bb_ref[...])
    mixed = sa * og_ref[...].astype(F32) + sb * od_ref[...].astype(F32)
    h = x_ref[...] + _dot(mixed.astype(BF16), wo_ref[...])
    hn = ((h * lax.rsqrt(jnp.mean(h * h, axis=-1, keepdims=True) + EPS)) * g2_ref[...]).astype(BF16)

    def up_cols(c0):
        cs = slice(c0, c0 + FFN_CHUNK)
        up = _dot(hn, wup_ref[:, cs])
        for k in range(FFN_CHUNK // LANES):
            blk = c0 // LANES + k
            ls = slice(k * LANES, (k + 1) * LANES)
            xx_ref[blk, 0:SUBLANES, :] = carry_ref[:, c0 + k * LANES:c0 + (k + 1) * LANES]
            xx_ref[blk, SUBLANES:SUBLANES + tm, :] = up[:, ls]
        carry_ref[:, cs] = up[tm - SUBLANES:tm, :]

    def conv_cols(c0):
        cols = []
        for k in range(FFN_CHUNK // LANES):
            blk = c0 // LANES + k
            ls = slice(c0 + k * LANES, c0 + (k + 1) * LANES)
            w = cw_ref[:, ls]
            groups = []
            for r in range(SUBLANES):
                y = cb_ref[:, ls]
                for j in range(FFN_CONV):
                    s0 = SUBLANES - (FFN_CONV - 1) + j + r
                    y = y + w[j:j + 1, :] * xx_ref[blk, pl.ds(s0, tm // SUBLANES, stride=SUBLANES), :]
                groups.append(y)
            cols.append(jnp.concatenate(groups, axis=0))
        return jnp.concatenate(cols, axis=1)

    n_chunk = D_FF // FFN_CHUNK
    up_cols(0)
    up_cols(D_FF)
    acc = jnp.zeros((tm, D_MODEL), F32)
    for c in range(n_chunk):
        if c + 1 < n_chunk:
            up_cols((c + 1) * FFN_CHUNK)
            up_cols(D_FF + (c + 1) * FFN_CHUNK)
        gate = conv_cols(c * FFN_CHUNK)
        val = conv_cols(D_FF + c * FFN_CHUNK)
        act = (_silu(gate) * val).astype(BF16)
        acc = acc + _dot(act, wdn_ref[c * FFN_CHUNK:(c + 1) * FFN_CHUNK, :])
    g_rows = tm // SUBLANES
    for k in range(D_MODEL // LANES):
        for r in range(SUBLANES):
            perm_ref[k, pl.ds(r, g_rows, stride=SUBLANES), :] = acc[r * g_rows:(r + 1) * g_rows,
                                                                    k * LANES:(k + 1) * LANES]
    h2 = h + jnp.concatenate([perm_ref[k] for k in range(D_MODEL // LANES)], axis=1)
    o_ref[...] = (h2 * lax.rsqrt(jnp.mean(h2 * h2, axis=-1, keepdims=True) + EPS)) * gf_ref[...]


def _mix_ffn(x, gates, bias_a, bias_b, o_gdn, o_dsa, w_out, g2, w_up, conv_w, conv_b, w_down, gf, *, tm=256):
    b, s, _ = x.shape
    const = lambda bi, t: (0, 0)
    row = lambda bi, t: (bi, t, 0)
    return pl.pallas_call(
        functools.partial(_mix_ffn_body, tm=tm),
        grid=(b, s // tm),
        in_specs=[
            pl.BlockSpec((None, tm, D_MODEL), row),
            pl.BlockSpec((None, tm, D_MODEL), lambda bi, t: (bi, t, 0)),
            pl.BlockSpec((None, tm, D_MODEL), lambda bi, t: (bi, t, (COL_GB - COL_GA) // D_MODEL)),
            pl.BlockSpec((1, D_MODEL), const),
            pl.BlockSpec((1, D_MODEL), const),
            pl.BlockSpec((None, tm, D_MODEL), row),
            pl.BlockSpec((None, tm, D_MODEL), row),
            pl.BlockSpec((D_MODEL, D_MODEL), const, pipeline_mode=pl.Buffered(1)),
            pl.BlockSpec((1, D_MODEL), const),
            pl.BlockSpec((D_MODEL, 2 * D_FF), const, pipeline_mode=pl.Buffered(1)),
            pl.BlockSpec((FFN_CONV, 2 * D_FF), const),
            pl.BlockSpec((1, 2 * D_FF), const),
            pl.BlockSpec((D_FF, D_MODEL), const, pipeline_mode=pl.Buffered(1)),
            pl.BlockSpec((1, D_MODEL), const),
        ],
        out_specs=pl.BlockSpec((None, tm, D_MODEL), lambda bi, t: (bi, t, 0)),
        out_shape=jax.ShapeDtypeStruct((b, s, D_MODEL), F32),
        scratch_shapes=[
            pltpu.VMEM((SUBLANES, 2 * D_FF), F32),
            pltpu.VMEM((2 * D_FF // LANES, SUBLANES + tm, LANES), F32),
            pltpu.VMEM((D_MODEL // LANES, tm, LANES), F32),
        ],
        compiler_params=_cparams(("arbitrary", "arbitrary")),
        name="mix_ffn",
    )(x, gates, gates, bias_a, bias_b, o_gdn, o_dsa, w_out, g2, w_up, conv_w, conv_b, w_down, gf)


def _rearranged_w_in(w):
    o = np.cumsum([0, 3072, 1024, 8, 8, 1024, 256, 256, 512, 64, 8, 1024, 1024])
    g_qkv, g_z, g_a, g_b, d_q, d_k, d_v, i_q, i_k, i_w, gate_a, gate_b = (
        w[:, int(o[i]):int(o[i + 1])] for i in range(12))
    big = jnp.concatenate([g_qkv, g_z, d_q, d_k, d_v, i_q, gate_a, gate_b], axis=1).astype(BF16)
    pad = jnp.zeros((w.shape[0], LANES - (IDX_DIM + 3 * 8)), w.dtype)
    small = jnp.concatenate([i_k, g_a, g_b, i_w, pad], axis=1).astype(BF16)
    return big, small


def _lane_vec(v, offset):
    return jnp.zeros((1, LANES), F32).at[0, offset:offset + v.shape[0]].set(v.astype(F32))


def kernel(x, norm1_g, w_in, gdn_conv_w, gdn_a_log, gdn_dt_bias, gdn_norm_g, idx_k_norm_g,
           branch_gate_b, w_out, norm2_g, ffn_w_up, ffn_conv_w, ffn_conv_b, ffn_w_down, final_g):
    b, s, d = x.shape
    depth = norm1_g.shape[0]
    assert depth == 1, "the final RMSNorm is fused into the (single) channel-mixer call"
    h = x
    for l in range(depth):
        w_big, w_small = _rearranged_w_in(w_in[l])
        outs = _inproj(h.reshape(b * s, d), norm1_g[l][None, :], w_big, w_small)
        gdn_in, dq, dk, dv, iq, gates, small = (o.reshape(b, s, -1) for o in outs)
        o_gdn = _gdn(gdn_in, small, gdn_conv_w[l], _lane_vec(gdn_a_log[l], SM_A),
                     _lane_vec(gdn_dt_bias[l], SM_A), gdn_norm_g[l][None, :].astype(F32))
        o_dsa = _dsa(dq, iq, dk, dv, small, _lane_vec(idx_k_norm_g[l], SM_IK))
        gb = branch_gate_b[l]
        h = _mix_ffn(h, gates, gb[None, :D_MODEL], gb[None, D_MODEL:], o_gdn, o_dsa, w_out[l].astype(BF16),
                     norm2_g[l][None, :], ffn_w_up[l].astype(BF16), ffn_conv_w[l],
                     ffn_conv_b[l][None, :], ffn_w_down[l].astype(BF16), final_g[None, :])
    return h
```

```python
import functools

import jax
import jax.numpy as jnp
import numpy as np
from jax import lax
from jax.experimental import pallas as pl
from jax.experimental.pallas import tpu as pltpu

F32 = jnp.float32
BF16 = jnp.bfloat16
I32 = jnp.int32

LANES = 128
SUBLANES = 8
VMEM_LIMIT = 56 * 1024 * 1024

D_MODEL = 1024
GDN_HEADS = 8
GDN_D = 128
GDN_CONV = 4
DSA_HEADS = 8
DSA_KV_HEADS = 2
DSA_DH = 128
IDX_HEADS = 8
IDX_DIM = 64
TOPK_MAX = 256
D_FF = 2816
FFN_CONV = 3
EPS = 1e-6

COL_QKV = 0
COL_Z = 3072
COL_DQ = 4096
COL_DK = 5120
COL_DV = 5376
COL_IQ = 5632
COL_GA = 6144
COL_GB = 7168
N_BIG = 8192
SM_IK = 0
SM_A = 64
SM_B = 72
SM_IW = 80

NEG_BIG = -1e30


def _cparams(sem):
    return pltpu.CompilerParams(dimension_semantics=sem, vmem_limit_bytes=VMEM_LIMIT)


def _dot(a, b):
    return jnp.dot(a, b, preferred_element_type=F32)


def _dot_nt(a, b):
    return lax.dot_general(a, b, (((1,), (1,)), ((), ())), preferred_element_type=F32)


def _dot_tn(a, b):
    return lax.dot_general(a, b, (((0,), (0,)), ((), ())), preferred_element_type=F32)


def _split3(x):
    h = x.astype(BF16)
    r = x - h.astype(F32)
    m = r.astype(BF16)
    l = (r - m.astype(F32)).astype(BF16)
    return h, m, l


def _sigmoid(x):
    return 1.0 / (1.0 + jnp.exp(-x))


def _silu(x):
    h = 0.5 * x
    return h + h * jnp.tanh(h)


def _softplus(x):
    return jnp.maximum(x, 0.0) + jnp.log(1.0 + jnp.exp(-jnp.abs(x)))


OUT_WIDTHS = (COL_DQ - COL_QKV,
              COL_DK - COL_DQ,
              COL_DV - COL_DK,
              COL_IQ - COL_DV,
              COL_GA - COL_IQ,
              N_BIG - COL_GA)


def _inproj_body(x_ref, g_ref, w_ref, ws_ref, *out_refs, col_chunk):
    small_ref = out_refs[-1]
    x = x_ref[...]
    ms = jnp.mean(x * x, axis=-1, keepdims=True)
    u = ((x * lax.rsqrt(ms + EPS)) * g_ref[...]).astype(BF16)
    starts = np.cumsum((0,) + OUT_WIDTHS)
    for c in range(N_BIG // col_chunk):
        c0 = c * col_chunk
        res = _dot(u, w_ref[:, c0:c0 + col_chunk]).astype(BF16)
        for ref, s0, width in zip(out_refs, starts, OUT_WIDTHS):
            lo, hi = max(c0, int(s0)), min(c0 + col_chunk, int(s0) + width)
            if lo < hi:
                ref[:, lo - int(s0):hi - int(s0)] = res[:, lo - c0:hi - c0]
    small_ref[...] = _dot(u, ws_ref[...])


def _inproj(x2, g, w_big, w_small, *, tm=512):
    n = x2.shape[0]
    const = lambda i: (0, 0)
    widths = OUT_WIDTHS + (LANES,)
    return pl.pallas_call(
        functools.partial(_inproj_body, col_chunk=1024),
        grid=(n // tm,),
        in_specs=[
            pl.BlockSpec((tm, D_MODEL), lambda i: (i, 0)),
            pl.BlockSpec((1, D_MODEL), const),
            pl.BlockSpec((D_MODEL, N_BIG), const, pipeline_mode=pl.Buffered(1)),
            pl.BlockSpec((D_MODEL, LANES), const, pipeline_mode=pl.Buffered(1)),
        ],
        out_specs=[pl.BlockSpec((tm, wd), lambda i: (i, 0)) for wd in widths],
        out_shape=[jax.ShapeDtypeStruct((n, wd), BF16) for wd in OUT_WIDTHS]
        + [jax.ShapeDtypeStruct((n, LANES), F32)],
        compiler_params=_cparams(("arbitrary",)),
        name="inproj",
    )(x2, g, w_big, w_small)


GDN_C = 128


def _tri_inverse(a_list):
    n = a_list[0].shape[0]
    row = lax.broadcasted_iota(I32, (n, n), 0)
    col = lax.broadcasted_iota(I32, (n, n), 1)
    eye = jnp.where(row == col, 1.0, 0.0).astype(F32)

    def same_block(size):
        sh = size.bit_length() - 1
        return (row >> sh) == (col >> sh)

    def square(ps):
        return [_dot(p, p).astype(BF16) for p in ps]

    def grow(ts, ps):
        return [t + _dot(t.astype(BF16), p) for t, p in zip(ts, ps)]

    p1 = [jnp.where(same_block(16), a, 0.0) for a in a_list]
    p2s = square([p.astype(BF16) for p in p1])
    ts = [eye - p for p in p1]
    p4s = square(p2s)
    ts = grow(ts, p2s)
    p8s = square(p4s)
    ts = grow(ts, p4s)
    ts = grow(ts, p8s)
    size = 16
    while size < n:
        in_pair = same_block(2 * size) & jnp.logical_not(same_block(size))
        offs = [jnp.where(in_pair, a, 0.0).astype(BF16) for a in a_list]
        tss = [t.astype(BF16) for t in ts]
        mids = [_dot(t2, o).astype(BF16) for t2, o in zip(tss, offs)]
        ts = [t - _dot(m, t2) for t, m, t2 in zip(ts, mids, tss)]
        size *= 2
    return ts


def _gdn_body(qp_ref, kp_ref, vp_ref, z_ref, sm_ref, wq_ref, wk_ref, wv_ref,
              alog_ref, dtb_ref, ng_ref, o_ref,
              state_ref, halo_ref, xx_ref, *, tb, hg):
    hgi = pl.program_id(1)
    t = pl.program_id(2)

    @pl.when(t == 0)
    def _():
        state_ref[...] = jnp.zeros_like(state_ref)
        halo_ref[...] = jnp.zeros_like(halo_ref)

    def conv_silu(idx, x_ref, w_ref):
        x = x_ref[...].astype(F32)
        xx_ref[0:SUBLANES, :] = halo_ref[idx]
        xx_ref[SUBLANES:SUBLANES + tb, :] = x
        halo_ref[idx] = x[tb - SUBLANES:tb, :]
        w = w_ref[...]
        y = w[GDN_CONV - 1:GDN_CONV, :] * x
        for j in range(GDN_CONV - 1):
            s0 = SUBLANES - (GDN_CONV - 1) + j
            y = y + w[j:j + 1, :] * xx_ref[s0:s0 + tb, :]
        return _silu(y)

    q_all = conv_silu(0, qp_ref, wq_ref)
    k_all = conv_silu(1, kp_ref, wk_ref)
    v_all = conv_silu(2, vp_ref, wv_ref)

    sm = sm_ref[...]
    ld_all = -jnp.exp(alog_ref[...]) * _softplus(sm + dtb_ref[...])
    beta_all = _sigmoid(sm)

    c = GDN_C
    row = lax.broadcasted_iota(I32, (c, c), 0)
    col = lax.broadcasted_iota(I32, (c, c), 1)
    lane = lax.broadcasted_iota(I32, (c, LANES), 1)
    tri_incl = jnp.where(row >= col, 1.0, 0.0).astype(BF16)
    ng = ng_ref[...]

    for ci in range(tb // c):
        rs = slice(ci * c, (ci + 1) * c)
        l1, l2, l3 = _split3(ld_all[rs])
        g_all = _dot(tri_incl, l1) + (_dot(tri_incl, l2) + _dot(tri_incl, l3))
        heads = range(hg)
        cols = [slice(j * GDN_D, (j + 1) * GDN_D) for j in heads]
        gc, bc, gamma, eg, g_last, qc, kc, kb, kc16 = ([None] * hg for _ in range(9))
        for j in heads:
            h = hgi * hg + j
            gcol = jnp.sum(jnp.where(lane == SM_A + h, g_all, 0.0), axis=-1, keepdims=True)
            bc[j] = jnp.sum(jnp.where(lane == SM_B + h, beta_all[rs], 0.0), axis=-1, keepdims=True)
            gc[j] = jnp.broadcast_to(gcol, (c, c))
            gamma[j] = jnp.exp(jnp.where(row >= col, gc[j] - gc[j].T, -jnp.inf))
            eg[j] = jnp.exp(gc[j])
            g_last[j] = gc[j][c - 1:c, :]
            q_h = q_all[rs, cols[j]]
            k_h = k_all[rs, cols[j]]
            qc[j] = q_h * (lax.rsqrt(jnp.sum(q_h * q_h, axis=-1, keepdims=True) + EPS) * (GDN_D ** -0.5))
            kc[j] = k_h * lax.rsqrt(jnp.sum(k_h * k_h, axis=-1, keepdims=True) + EPS)
            kb[j] = kc[j] * bc[j]
            kc16[j] = kc[j].astype(BF16)
        a = [jnp.where(row > col, _dot_nt(kb[j].astype(BF16), kc16[j]) * gamma[j], 0.0) for j in heads]
        att = [(_dot_nt(qc[j].astype(BF16), kc16[j]) * gamma[j]).astype(BF16) for j in heads]
        tinv = [t.astype(BF16) for t in _tri_inverse(a)]
        u = [_dot(tinv[j], (v_all[rs, cols[j]] * bc[j]).astype(BF16)) for j in heads]
        w = [_dot(tinv[j], (kb[j] * eg[j]).astype(BF16)).astype(BF16) for j in heads]
        qg = [(qc[j] * eg[j]).astype(BF16) for j in heads]
        kd = [(kc[j] * jnp.exp(g_last[j] - gc[j])).astype(BF16) for j in heads]

        s_old = [state_ref[j] for j in heads]
        s16 = [s.astype(BF16) for s in s_old]
        v_new16 = [(u[j] - _dot(w[j], s16[j])).astype(BF16) for j in heads]
        o = [_dot(qg[j], s16[j]) + _dot(att[j], v_new16[j]) for j in heads]
        for j in heads:
            state_ref[j] = s_old[j] * jnp.exp(g_last[j]) + _dot_tn(kd[j], v_new16[j])
        for j in heads:
            y = o[j] * lax.rsqrt(jnp.mean(o[j] * o[j], axis=-1, keepdims=True) + EPS) * ng
            o_ref[rs, cols[j]] = (y * _silu(z_ref[rs, cols[j]].astype(F32))).astype(o_ref.dtype)


def _gdn(proj, small, conv_w, alog_v, dtb_v, ng, *, tb=256, hg=8):
    b, s, _ = proj.shape
    nh = GDN_HEADS
    w = hg * GDN_D
    ng_blocks = nh // hg
    hq = COL_QKV // w
    hz = COL_Z // w
    return pl.pallas_call(
        functools.partial(_gdn_body, tb=tb, hg=hg),
        grid=(b, ng_blocks, s // tb),
        in_specs=[
            pl.BlockSpec((None, tb, w), lambda bi, g, t: (bi, t, hq + g)),
            pl.BlockSpec((None, tb, w), lambda bi, g, t: (bi, t, hq + ng_blocks + g)),
            pl.BlockSpec((None, tb, w), lambda bi, g, t: (bi, t, hq + 2 * ng_blocks + g)),
            pl.BlockSpec((None, tb, w), lambda bi, g, t: (bi, t, hz + g)),
            pl.BlockSpec((None, tb, LANES), lambda bi, g, t: (bi, t, 0)),
            pl.BlockSpec((GDN_CONV, w), lambda bi, g, t: (0, g)),
            pl.BlockSpec((GDN_CONV, w), lambda bi, g, t: (0, ng_blocks + g)),
            pl.BlockSpec((GDN_CONV, w), lambda bi, g, t: (0, 2 * ng_blocks + g)),
            pl.BlockSpec((1, LANES), lambda bi, g, t: (0, 0)),
            pl.BlockSpec((1, LANES), lambda bi, g, t: (0, 0)),
            pl.BlockSpec((1, LANES), lambda bi, g, t: (0, 0)),
        ],
        out_specs=pl.BlockSpec((None, tb, w), lambda bi, g, t: (bi, t, g)),
        out_shape=jax.ShapeDtypeStruct((b, s, nh * GDN_D), BF16),
        scratch_shapes=[
            pltpu.VMEM((hg, GDN_D, GDN_D), F32),
            pltpu.VMEM((3, SUBLANES, w), F32),
            pltpu.VMEM((SUBLANES + tb, w), F32),
        ],
        compiler_params=_cparams(("arbitrary", "arbitrary", "arbitrary")),
        name="gdn",
    )(proj, proj, proj, proj, small, conv_w, conv_w, conv_w, alog_v, dtb_v, ng)


DSA_QB = 128
DSA_KC = 512
DSA_ONES = 16
_KEY_NEG_INF = np.int32(np.array(0xFF800000, np.uint32).view(np.int32) ^ np.int32(0x7FFFFFFF))
_KEY_POS_INF_P1 = np.int32(0x7F800001)


def _key_to_f32(key):
    bits = key ^ ((key >> 31) & jnp.int32(0x7FFFFFFF))
    return lax.bitcast_convert_type(bits, F32)


def _tree_sum(xs):
    xs = list(xs)
    while len(xs) > 1:
        nxt = [xs[i] + xs[i + 1] for i in range(0, len(xs) - 1, 2)]
        if len(xs) % 2:
            nxt.append(xs[-1])
        xs = nxt
    return xs[0]


def _count_rows(hit):
    rows, n = hit.shape
    parts = hit.reshape(rows // SUBLANES, SUBLANES, n)
    return _tree_sum([parts[i] for i in range(rows // SUBLANES)])


def _dsa_body(q_ref, iq_ref, smq_ref, k_ref, v_ref, smk_ref, ikg_ref, o_ref,
              ikn_ref, vt_ref, s_ref, thrkey_ref, cnt_ref, jsel_ref, m_ref, alpha_ref, acc_ref, lg_ref, *, seq, k_top):
    qb = pl.program_id(1)
    kc = DSA_KC
    nq = DSA_QB
    rep = DSA_HEADS // DSA_KV_HEADS

    @pl.when(qb == 0)
    def _():
        def prep(i, carry):
            rs = pl.ds(pl.multiple_of(i * kc, kc), kc)
            sm = smk_ref[rs, :]
            lane = lax.broadcasted_iota(I32, sm.shape, 1)
            isk = lane < IDX_DIM
            mean = jnp.sum(jnp.where(isk, sm, 0.0), axis=-1, keepdims=True) * (1.0 / IDX_DIM)
            xc = jnp.where(isk, sm - mean, 0.0)
            var = jnp.sum(xc * xc, axis=-1, keepdims=True) * (1.0 / IDX_DIM)
            y = xc * lax.rsqrt(var + EPS) * ikg_ref[...]
            ikn_ref[0, rs, :] = y.astype(BF16)
            ikn_ref[1, rs, :] = pltpu.roll(y, IDX_DIM, axis=1).astype(BF16)
            vv = v_ref[rs, :].astype(F32)
            for g in range(DSA_KV_HEADS):
                vt_ref[g, i, 0:DSA_DH, :] = vv[:, g * DSA_DH:(g + 1) * DSA_DH].T.astype(BF16)
                vt_ref[g, i, DSA_DH:DSA_DH + DSA_ONES, :] = jnp.ones((DSA_ONES, kc), BF16)
            return carry
        lax.fori_loop(0, seq // kc, prep, 0)

    q_lo = qb * nq
    n_kc = (q_lo + nq + kc - 1) // kc
    qpos = q_lo + lax.broadcasted_iota(I32, (1, nq), 1)

    iq = iq_ref[...]
    n_pair = IDX_HEADS // 2
    iq_t = jnp.concatenate([iq[:, p * LANES:(p + 1) * LANES].astype(F32).T for p in range(n_pair)],
                           axis=1).astype(BF16)
    iw_t = smq_ref[...].T * ((IDX_HEADS ** -0.5) * (IDX_DIM ** -0.5))

    def score_chunk(i, carry):
        rs = pl.ds(pl.multiple_of(i * kc, kc), kc)
        lo = _dot(ikn_ref[0, rs, :], iq_t)
        hi = _dot(ikn_ref[1, rs, :], iq_t)
        sc = jnp.zeros((kc, nq), F32)
        for p in range(n_pair):
            cs = slice(p * nq, (p + 1) * nq)
            sc = sc + iw_t[SM_IW + 2 * p:SM_IW + 2 * p + 1, :] * jnp.maximum(lo[:, cs], 0.0)
            sc = sc + iw_t[SM_IW + 2 * p + 1:SM_IW + 2 * p + 2, :] * jnp.maximum(hi[:, cs], 0.0)
        kpos = i * kc + lax.broadcasted_iota(I32, (kc, nq), 0)
        s_ref[rs, :] = jnp.where(kpos <= qpos, sc + 0.0, -jnp.inf)
        return carry
    lax.fori_loop(0, n_kc, score_chunk, 0)

    kf = jnp.float32(k_top)

    def midpoint(lo, hi):
        return (lo >> 1) + (hi >> 1) + (lo & hi & 1)

    def bisect_static(rows):
        pieces = [slice(r0, min(r0 + kc, rows)) for r0 in range(0, rows, kc)]

        def count(hit_fn):
            parts = [_count_rows(jnp.where(hit_fn(s_ref[rs, :]), 1.0, 0.0)) for rs in pieces]
            return jnp.sum(_tree_sum(parts), axis=0, keepdims=True)

        def count_ge(thr):
            return count(lambda s: s >= thr)

        def bisect(_, carry):
            lo, hi = carry
            mid = midpoint(lo, hi)
            ok = count_ge(_key_to_f32(mid)) >= kf
            return jnp.where(ok, mid, lo), jnp.where(ok, hi, mid)
        lo0 = jnp.full((1, nq), _KEY_NEG_INF, I32)
        hi0 = jnp.full((1, nq), _KEY_POS_INF_P1, I32)
        lo, _ = lax.fori_loop(0, 32, bisect, (lo0, hi0))
        thrkey_ref[...] = lo
        thr = _key_to_f32(lo)
        cnt_ref[0] = count(lambda s: s > thr)
        cnt_ref[1] = count_ge(thr)

    for j in range(seq // nq):
        pl.when(qb == j)(functools.partial(bisect_static, (j + 1) * nq))
    thr = _key_to_f32(thrkey_ref[...])

    n_gt = cnt_ref[0]
    n_ge = cnt_ref[1]
    need = kf - n_gt
    jsel_ref[...] = jnp.full((1, nq), seq, I32)

    @pl.when(jnp.max(n_ge) > kf)
    def _():
        def count_eq_upto(jmax):
            def body(i, cnt):
                rs = pl.ds(pl.multiple_of(i * kc, kc), kc)
                kpos = i * kc + lax.broadcasted_iota(I32, (kc, nq), 0)
                hit = jnp.where(s_ref[rs, :] == thr, jnp.where(kpos <= jmax, 1.0, 0.0), 0.0)
                return cnt + _count_rows(hit)
            cnt = lax.fori_loop(0, n_kc, body, jnp.zeros((SUBLANES, nq), F32))
            return jnp.sum(cnt, axis=0, keepdims=True)

        def jbisect(_, carry):
            lo_j, hi_j = carry
            mid = (lo_j + hi_j) >> 1
            ok = count_eq_upto(mid) >= need
            return jnp.where(ok, lo_j, mid), jnp.where(ok, mid, hi_j)
        n_iter = int(np.ceil(np.log2(seq))) + 1
        lo_j0 = jnp.full((1, nq), -1, I32)
        hi_j0 = jnp.full((1, nq), seq - 1, I32)
        _, hi_j = lax.fori_loop(0, n_iter, jbisect, (lo_j0, hi_j0))
        jsel_ref[...] = hi_j

    jsel = jsel_ref[...]

    def selection_bias(i):
        s = s_ref[pl.ds(pl.multiple_of(i * kc, kc), kc), :]
        kpos = i * kc + lax.broadcasted_iota(I32, (kc, nq), 0)
        tie = jnp.where(s == thr, jnp.where(kpos <= jsel, 0.0, NEG_BIG), NEG_BIG)
        sel = jnp.where(s > thr, 0.0, tie)
        return jnp.where(kpos <= qpos, sel, NEG_BIG)

    qall = q_ref[...].astype(F32) * (DSA_DH ** -0.5 * np.log2(np.e))
    q_t = [jnp.concatenate([qall[:, (g * rep + r) * DSA_DH:(g * rep + r + 1) * DSA_DH].T
                            for r in range(rep)], axis=1).astype(BF16) for g in range(DSA_KV_HEADS)]
    m_ref[...] = jnp.full(m_ref.shape, NEG_BIG, F32)
    acc_ref[...] = jnp.zeros(acc_ref.shape, F32)

    def logits_phase(i):
        rs = pl.ds(pl.multiple_of(i * kc, kc), kc)
        bias_w = jnp.concatenate([selection_bias(i)] * rep, axis=1)
        for g in range(DSA_KV_HEADS):
            logit = _dot(k_ref[rs, g * DSA_DH:(g + 1) * DSA_DH], q_t[g]) + bias_w
            lg_ref[g] = logit
            m_old = m_ref[g]
            m_new = jnp.maximum(m_old, jnp.max(logit, axis=0, keepdims=True))
            alpha_ref[g] = jnp.exp2(m_old - m_new)
            m_ref[g] = m_new

    def value_phase(i):
        for g in range(DSA_KV_HEADS):
            p = jnp.exp2(lg_ref[g] - m_ref[g]).astype(BF16)
            acc_ref[g] = acc_ref[g] * alpha_ref[g] + _dot(vt_ref[g, i], p)

    logits_phase(0)

    def attn_chunk(i, carry):
        value_phase(i - 1)
        logits_phase(i)
        return carry
    lax.fori_loop(1, n_kc, attn_chunk, 0)
    value_phase(n_kc - 1)

    for g in range(DSA_KV_HEADS):
        acc = acc_ref[g]
        ot = acc[0:DSA_DH] / acc[DSA_DH:DSA_DH + 1]
        for r in range(rep):
            hh = g * rep + r
            o_ref[:, hh * DSA_DH:(hh + 1) * DSA_DH] = ot[:, r * nq:(r + 1) * nq].T.astype(o_ref.dtype)


def _dsa(dq, iq, dk, dv, small, ikg):
    b, s, _ = dq.shape
    nq = DSA_QB
    k_top = min(TOPK_MAX, s // 4)
    rep = DSA_HEADS // DSA_KV_HEADS
    kvw = DSA_KV_HEADS * DSA_DH
    return pl.pallas_call(
        functools.partial(_dsa_body, seq=s, k_top=k_top),
        grid=(b, s // nq),
        in_specs=[
            pl.BlockSpec((None, nq, DSA_HEADS * DSA_DH), lambda bi, qb: (bi, qb, 0)),
            pl.BlockSpec((None, nq, IDX_HEADS * IDX_DIM), lambda bi, qb: (bi, qb, 0)),
            pl.BlockSpec((None, nq, LANES), lambda bi, qb: (bi, qb, 0)),
            pl.BlockSpec((None, s, kvw), lambda bi, qb: (bi, 0, 0)),
            pl.BlockSpec((None, s, kvw), lambda bi, qb: (bi, 0, 0)),
            pl.BlockSpec((None, s, LANES), lambda bi, qb: (bi, 0, 0)),
            pl.BlockSpec((1, LANES), lambda bi, qb: (0, 0)),
        ],
        out_specs=pl.BlockSpec((None, nq, DSA_HEADS * DSA_DH), lambda bi, qb: (bi, qb, 0)),
        out_shape=jax.ShapeDtypeStruct((b, s, DSA_HEADS * DSA_DH), BF16),
        scratch_shapes=[
            pltpu.VMEM((2, s, LANES), BF16),
            pltpu.VMEM((DSA_KV_HEADS, s // DSA_KC, DSA_DH + DSA_ONES, DSA_KC), BF16),
            pltpu.VMEM((s, nq), F32),
            pltpu.VMEM((1, nq), I32),
            pltpu.VMEM((2, 1, nq), F32),
            pltpu.VMEM((1, nq), I32),
            pltpu.VMEM((DSA_KV_HEADS, 1, rep * nq), F32),
            pltpu.VMEM((DSA_KV_HEADS, 1, rep * nq), F32),
            pltpu.VMEM((DSA_KV_HEADS, DSA_DH + DSA_ONES, rep * nq), F32),
            pltpu.VMEM((DSA_KV_HEADS, DSA_KC, rep * nq), F32),
        ],
        compiler_params=_cparams(("arbitrary", "arbitrary")),
        name="dsa",
    )(dq, iq, small, dk, dv, small, ikg)


FFN_CHUNK = 256


def _mix_ffn_body(x_ref, ga_ref, gb_ref, ba_ref, bb_ref, og_ref, od_ref, wo_ref,
                  g2_ref, wup_ref, cw_ref, cb_ref, wdn_ref, gf_ref, o_ref,
                  carry_ref, xx_ref, perm_ref, *, tm):
    t = pl.program_id(1)

    @pl.when(t == 0)
    def _():
        carry_ref[...] = jnp.zeros_like(carry_ref)

    sa = _sigmoid(ga_ref[...].astype(F32) + ba_ref[...])
    sb = _sigmoid(gb_ref[...].astype(F32) + bb_ref[...])
    mixed = sa * og_ref[...].astype(F32) + sb * od_ref[...].astype(F32)
    h = x_ref[...] + _dot(mixed.astype(BF16), wo_ref[...])
    hn = ((h * lax.rsqrt(jnp.mean(h * h, axis=-1, keepdims=True) + EPS)) * g2_ref[...]).astype(BF16)

    def up_cols(c0):
        cs = slice(c0, c0 + FFN_CHUNK)
        up = _dot(hn, wup_ref[:, cs])
        for k in range(FFN_CHUNK // LANES):
            blk = c0 // LANES + k
            ls = slice(k * LANES, (k + 1) * LANES)
            xx_ref[blk, 0:SUBLANES, :] = carry_ref[:, c0 + k * LANES:c0 + (k + 1) * LANES]
            xx_ref[blk, SUBLANES:SUBLANES + tm, :] = up[:, ls]
        carry_ref[:, cs] = up[tm - SUBLANES:tm, :]

    def conv_cols(c0):
        cols = []
        for k in range(FFN_CHUNK // LANES):
            blk = c0 // LANES + k
            ls = slice(c0 + k * LANES, c0 + (k + 1) * LANES)
            w = cw_ref[:, ls]
            groups = []
            for r in range(SUBLANES):
                y = cb_ref[:, ls]
                for j in range(FFN_CONV):
                    s0 = SUBLANES - (FFN_CONV - 1) + j + r
                    y = y + w[j:j + 1, :] * xx_ref[blk, pl.ds(s0, tm // SUBLANES, stride=SUBLANES), :]
                groups.append(y)
            cols.append(jnp.concatenate(groups, axis=0))
        return jnp.concatenate(cols, axis=1)

    n_chunk = D_FF // FFN_CHUNK
    up_cols(0)
    up_cols(D_FF)
    acc = jnp.zeros((tm, D_MODEL), F32)
    for c in range(n_chunk):
        if c + 1 < n_chunk:
            up_cols((c + 1) * FFN_CHUNK)
            up_cols(D_FF + (c + 1) * FFN_CHUNK)
        gate = conv_cols(c * FFN_CHUNK)
        val = conv_cols(D_FF + c * FFN_CHUNK)
        act = (_silu(gate) * val).astype(BF16)
        acc = acc + _dot(act, wdn_ref[c * FFN_CHUNK:(c + 1) * FFN_CHUNK, :])
    g_rows = tm // SUBLANES
    for k in range(D_MODEL // LANES):
        for r in range(SUBLANES):
            perm_ref[k, pl.ds(r, g_rows, stride=SUBLANES), :] = acc[r * g_rows:(r + 1) * g_rows,
                                                                    k * LANES:(k + 1) * LANES]
    h2 = h + jnp.concatenate([perm_ref[k] for k in range(D_MODEL // LANES)], axis=1)
    o_ref[...] = (h2 * lax.rsqrt(jnp.mean(h2 * h2, axis=-1, keepdims=True) + EPS)) * gf_ref[...]


def _mix_ffn(x, gates, bias_a, bias_b, o_gdn, o_dsa, w_out, g2, w_up, conv_w, conv_b, w_down, gf, *, tm=256):
    b, s, _ = x.shape
    const = lambda bi, t: (0, 0)
    row = lambda bi, t: (bi, t, 0)
    return pl.pallas_call(
        functools.partial(_mix_ffn_body, tm=tm),
        grid=(b, s // tm),
        in_specs=[
            pl.BlockSpec((None, tm, D_MODEL), row),
            pl.BlockSpec((None, tm, D_MODEL), lambda bi, t: (bi, t, 0)),
            pl.BlockSpec((None, tm, D_MODEL), lambda bi, t: (bi, t, (COL_GB - COL_GA) // D_MODEL)),
            pl.BlockSpec((1, D_MODEL), const),
            pl.BlockSpec((1, D_MODEL), const),
            pl.BlockSpec((None, tm, D_MODEL), row),
            pl.BlockSpec((None, tm, D_MODEL), row),
            pl.BlockSpec((D_MODEL, D_MODEL), const, pipeline_mode=pl.Buffered(1)),
            pl.BlockSpec((1, D_MODEL), const),
            pl.BlockSpec((D_MODEL, 2 * D_FF), const, pipeline_mode=pl.Buffered(1)),
            pl.BlockSpec((FFN_CONV, 2 * D_FF), const),
            pl.BlockSpec((1, 2 * D_FF), const),
            pl.BlockSpec((D_FF, D_MODEL), const, pipeline_mode=pl.Buffered(1)),
            pl.BlockSpec((1, D_MODEL), const),
        ],
        out_specs=pl.BlockSpec((None, tm, D_MODEL), lambda bi, t: (bi, t, 0)),
        out_shape=jax.ShapeDtypeStruct((b, s, D_MODEL), F32),
        scratch_shapes=[
            pltpu.VMEM((SUBLANES, 2 * D_FF), F32),
            pltpu.VMEM((2 * D_FF // LANES, SUBLANES + tm, LANES), F32),
            pltpu.VMEM((D_MODEL // LANES, tm, LANES), F32),
        ],
        compiler_params=_cparams(("arbitrary", "arbitrary")),
        name="mix_ffn",
    )(x, gates, gates, bias_a, bias_b, o_gdn, o_dsa, w_out, g2, w_up, conv_w, conv_b, w_down, gf)


def _rearranged_w_in(w):
    o = np.cumsum([0, 3072, 1024, 8, 8, 1024, 256, 256, 512, 64, 8, 1024, 1024])
    g_qkv, g_z, g_a, g_b, d_q, d_k, d_v, i_q, i_k, i_w, gate_a, gate_b = (
        w[:, int(o[i]):int(o[i + 1])] for i in range(12))
    big = jnp.concatenate([g_qkv, g_z, d_q, d_k, d_v, i_q, gate_a, gate_b], axis=1).astype(BF16)
    pad = jnp.zeros((w.shape[0], LANES - (IDX_DIM + 3 * 8)), w.dtype)
    small = jnp.concatenate([i_k, g_a, g_b, i_w, pad], axis=1).astype(BF16)
    return big, small


def _lane_vec(v, offset):
    return jnp.zeros((1, LANES), F32).at[0, offset:offset + v.shape[0]].set(v.astype(F32))


def kernel(x, norm1_g, w_in, gdn_conv_w, gdn_a_log, gdn_dt_bias, gdn_norm_g, idx_k_norm_g,
           branch_gate_b, w_out, norm2_g, ffn_w_up, ffn_conv_w, ffn_conv_b, ffn_w_down, final_g):
    b, s, d = x.shape
    depth = norm1_g.shape[0]
    assert depth == 1, "the final RMSNorm is fused into the (single) channel-mixer call"
    h = x
    for l in range(depth):
        w_big, w_small = _rearranged_w_in(w_in[l])
        outs = _inproj(h.reshape(b * s, d), norm1_g[l][None, :], w_big, w_small)
        gdn_in, dq, dk, dv, iq, gates, small = (o.reshape(b, s, -1) for o in outs)
        o_gdn = _gdn(gdn_in, small, gdn_conv_w[l], _lane_vec(gdn_a_log[l], SM_A),
                     _lane_vec(gdn_dt_bias[l], SM_A), gdn_norm_g[l][None, :].astype(F32))
        o_dsa = _dsa(dq, iq, dk, dv, small, _lane_vec(idx_k_norm_g[l], SM_IK))
        gb = branch_gate_b[l]
        h = _mix_ffn(h, gates, gb[None, :D_MODEL], gb[None, D_MODEL:], o_gdn, o_dsa, w_out[l].astype(BF16),
                     norm2_g[l][None, :], ffn_w_up[l].astype(BF16), ffn_conv_w[l],
                     ffn_conv_b[l][None, :], ffn_w_down[l].astype(BF16), final_g[None, :])
    return h
```

```python
import functools

import jax
import jax.numpy as jnp
import numpy as np
from jax import lax
from jax.experimental import pallas as pl
from jax.experimental.pallas import tpu as pltpu

F32 = jnp.float32
BF16 = jnp.bfloat16
I32 = jnp.int32

LANES = 128
SUBLANES = 8
VMEM_LIMIT = 56 * 1024 * 1024

D_MODEL = 1024
GDN_HEADS = 8
GDN_D = 128
GDN_CONV = 4
DSA_HEADS = 8
DSA_KV_HEADS = 2
DSA_DH = 128
IDX_HEADS = 8
IDX_DIM = 64
TOPK_MAX = 256
D_FF = 2816
FFN_CONV = 3
EPS = 1e-6

COL_QKV = 0
COL_Z = 3072
COL_DQ = 4096
COL_DK = 5120
COL_DV = 5376
COL_IQ = 5632
COL_GA = 6144
COL_GB = 7168
N_BIG = 8192
SM_IK = 0
SM_A = 64
SM_B = 72
SM_IW = 80

NEG_BIG = -1e30


def _cparams(sem):
    return pltpu.CompilerParams(dimension_semantics=sem, vmem_limit_bytes=VMEM_LIMIT)


def _dot(a, b):
    return jnp.dot(a, b, preferred_element_type=F32)


def _dot_nt(a, b):
    return lax.dot_general(a, b, (((1,), (1,)), ((), ())), preferred_element_type=F32)


def _dot_tn(a, b):
    return lax.dot_general(a, b, (((0,), (0,)), ((), ())), preferred_element_type=F32)


def _split3(x):
    h = x.astype(BF16)
    r = x - h.astype(F32)
    m = r.astype(BF16)
    l = (r - m.astype(F32)).astype(BF16)
    return h, m, l


def _sigmoid(x):
    return 1.0 / (1.0 + jnp.exp(-x))


def _silu(x):
    h = 0.5 * x
    return h + h * jnp.tanh(h)


def _softplus(x):
    return jnp.maximum(x, 0.0) + jnp.log(1.0 + jnp.exp(-jnp.abs(x)))


OUT_WIDTHS = (COL_DQ - COL_QKV,
              COL_DK - COL_DQ,
              COL_DV - COL_DK,
              COL_IQ - COL_DV,
              COL_GA - COL_IQ,
              N_BIG - COL_GA)


def _inproj_body(x_ref, g_ref, w_ref, ws_ref, *out_refs, col_chunk):
    small_ref = out_refs[-1]
    x = x_ref[...]
    ms = jnp.mean(x * x, axis=-1, keepdims=True)
    u = ((x * lax.rsqrt(ms + EPS)) * g_ref[...]).astype(BF16)
    starts = np.cumsum((0,) + OUT_WIDTHS)
    for c in range(N_BIG // col_chunk):
        c0 = c * col_chunk
        res = _dot(u, w_ref[:, c0:c0 + col_chunk]).astype(BF16)
        for ref, s0, width in zip(out_refs, starts, OUT_WIDTHS):
            lo, hi = max(c0, int(s0)), min(c0 + col_chunk, int(s0) + width)
            if lo < hi:
                ref[:, lo - int(s0):hi - int(s0)] = res[:, lo - c0:hi - c0]
    small_ref[...] = _dot(u, ws_ref[...])


def _inproj(x2, g, w_big, w_small, *, tm=512):
    n = x2.shape[0]
    const = lambda i: (0, 0)
    widths = OUT_WIDTHS + (LANES,)
    return pl.pallas_call(
        functools.partial(_inproj_body, col_chunk=1024),
        grid=(n // tm,),
        in_specs=[
            pl.BlockSpec((tm, D_MODEL), lambda i: (i, 0)),
            pl.BlockSpec((1, D_MODEL), const),
            pl.BlockSpec((D_MODEL, N_BIG), const, pipeline_mode=pl.Buffered(1)),
            pl.BlockSpec((D_MODEL, LANES), const, pipeline_mode=pl.Buffered(1)),
        ],
        out_specs=[pl.BlockSpec((tm, wd), lambda i: (i, 0)) for wd in widths],
        out_shape=[jax.ShapeDtypeStruct((n, wd), BF16) for wd in OUT_WIDTHS]
        + [jax.ShapeDtypeStruct((n, LANES), F32)],
        compiler_params=_cparams(("arbitrary",)),
        name="inproj",
    )(x2, g, w_big, w_small)


GDN_C = 128


def _tri_inverse(a_list):
    n = a_list[0].shape[0]
    row = lax.broadcasted_iota(I32, (n, n), 0)
    col = lax.broadcasted_iota(I32, (n, n), 1)
    eye = jnp.where(row == col, 1.0, 0.0).astype(F32)

    def same_block(size):
        sh = size.bit_length() - 1
        return (row >> sh) == (col >> sh)

    def square(ps):
        return [_dot(p, p).astype(BF16) for p in ps]

    def grow(ts, ps):
        return [t + _dot(t.astype(BF16), p) for t, p in zip(ts, ps)]

    p1 = [jnp.where(same_block(16), a, 0.0) for a in a_list]
    p2s = square([p.astype(BF16) for p in p1])
    ts = [eye - p for p in p1]
    p4s = square(p2s)
    ts = grow(ts, p2s)
    p8s = square(p4s)
    ts = grow(ts, p4s)
    ts = grow(ts, p8s)
    size = 16
    while size < n:
        in_pair = same_block(2 * size) & jnp.logical_not(same_block(size))
        offs = [jnp.where(in_pair, a, 0.0).astype(BF16) for a in a_list]
        tss = [t.astype(BF16) for t in ts]
        mids = [_dot(t2, o).astype(BF16) for t2, o in zip(tss, offs)]
        ts = [t - _dot(m, t2) for t, m, t2 in zip(ts, mids, tss)]
        size *= 2
    return ts


def _gdn_body(qp_ref, kp_ref, vp_ref, z_ref, sm_ref, wq_ref, wk_ref, wv_ref,
              alog_ref, dtb_ref, ng_ref, o_ref,
              state_ref, halo_ref, xx_ref, *, tb, hg):
    hgi = pl.program_id(1)
    t = pl.program_id(2)

    @pl.when(t == 0)
    def _():
        state_ref[...] = jnp.zeros_like(state_ref)
        halo_ref[...] = jnp.zeros_like(halo_ref)

    def conv_silu(idx, x_ref, w_ref):
        x = x_ref[...].astype(F32)
        xx_ref[0:SUBLANES, :] = halo_ref[idx]
        xx_ref[SUBLANES:SUBLANES + tb, :] = x
        halo_ref[idx] = x[tb - SUBLANES:tb, :]
        w = w_ref[...]
        y = w[GDN_CONV - 1:GDN_CONV, :] * x
        for j in range(GDN_CONV - 1):
            s0 = SUBLANES - (GDN_CONV - 1) + j
            y = y + w[j:j + 1, :] * xx_ref[s0:s0 + tb, :]
        return _silu(y)

    q_all = conv_silu(0, qp_ref, wq_ref)
    k_all = conv_silu(1, kp_ref, wk_ref)
    v_all = conv_silu(2, vp_ref, wv_ref)

    sm = sm_ref[...]
    ld_all = -jnp.exp(alog_ref[...]) * _softplus(sm + dtb_ref[...])
    beta_all = _sigmoid(sm)

    c = GDN_C
    row = lax.broadcasted_iota(I32, (c, c), 0)
    col = lax.broadcasted_iota(I32, (c, c), 1)
    lane = lax.broadcasted_iota(I32, (c, LANES), 1)
    tri_incl = jnp.where(row >= col, 1.0, 0.0).astype(BF16)
    ng = ng_ref[...]

    for ci in range(tb // c):
        rs = slice(ci * c, (ci + 1) * c)
        l1, l2, l3 = _split3(ld_all[rs])
        g_all = _dot(tri_incl, l1) + (_dot(tri_incl, l2) + _dot(tri_incl, l3))
        heads = range(hg)
        cols = [slice(j * GDN_D, (j + 1) * GDN_D) for j in heads]
        gc, bc, gamma, eg, g_last, qc, kc, kb, kc16 = ([None] * hg for _ in range(9))
        for j in heads:
            h = hgi * hg + j
            gcol = jnp.sum(jnp.where(lane == SM_A + h, g_all, 0.0), axis=-1, keepdims=True)
            bc[j] = jnp.sum(jnp.where(lane == SM_B + h, beta_all[rs], 0.0), axis=-1, keepdims=True)
            gc[j] = jnp.broadcast_to(gcol, (c, c))
            gamma[j] = jnp.exp(jnp.where(row >= col, gc[j] - gc[j].T, -jnp.inf))
            eg[j] = jnp.exp(gc[j])
            g_last[j] = gc[j][c - 1:c, :]
            q_h = q_all[rs, cols[j]]
            k_h = k_all[rs, cols[j]]
            qc[j] = q_h * (lax.rsqrt(jnp.sum(q_h * q_h, axis=-1, keepdims=True) + EPS) * (GDN_D ** -0.5))
            kc[j] = k_h * lax.rsqrt(jnp.sum(k_h * k_h, axis=-1, keepdims=True) + EPS)
            kb[j] = kc[j] * bc[j]
            kc16[j] = kc[j].astype(BF16)
        a = [jnp.where(row > col, _dot_nt(kb[j].astype(BF16), kc16[j]) * gamma[j], 0.0) for j in heads]
        att = [(_dot_nt(qc[j].astype(BF16), kc16[j]) * gamma[j]).astype(BF16) for j in heads]
        tinv = [t.astype(BF16) for t in _tri_inverse(a)]
        u = [_dot(tinv[j], (v_all[rs, cols[j]] * bc[j]).astype(BF16)) for j in heads]
        w = [_dot(tinv[j], (kb[j] * eg[j]).astype(BF16)).astype(BF16) for j in heads]
        qg = [(qc[j] * eg[j]).astype(BF16) for j in heads]
        kd = [(kc[j] * jnp.exp(g_last[j] - gc[j])).astype(BF16) for j in heads]

        s_old = [state_ref[j] for j in heads]
        s16 = [s.astype(BF16) for s in s_old]
        v_new16 = [(u[j] - _dot(w[j], s16[j])).astype(BF16) for j in heads]
        o = [_dot(qg[j], s16[j]) + _dot(att[j], v_new16[j]) for j in heads]
        for j in heads:
            state_ref[j] = s_old[j] * jnp.exp(g_last[j]) + _dot_tn(kd[j], v_new16[j])
        for j in heads:
            y = o[j] * lax.rsqrt(jnp.mean(o[j] * o[j], axis=-1, keepdims=True) + EPS) * ng
            o_ref[rs, cols[j]] = (y * _silu(z_ref[rs, cols[j]].astype(F32))).astype(o_ref.dtype)


def _gdn(proj, small, conv_w, alog_v, dtb_v, ng, *, tb=512, hg=8):
    b, s, _ = proj.shape
    nh = GDN_HEADS
    w = hg * GDN_D
    ng_blocks = nh // hg
    hq = COL_QKV // w
    hz = COL_Z // w
    return pl.pallas_call(
        functools.partial(_gdn_body, tb=tb, hg=hg),
        grid=(b, ng_blocks, s // tb),
        in_specs=[
            pl.BlockSpec((None, tb, w), lambda bi, g, t: (bi, t, hq + g)),
            pl.BlockSpec((None, tb, w), lambda bi, g, t: (bi, t, hq + ng_blocks + g)),
            pl.BlockSpec((None, tb, w), lambda bi, g, t: (bi, t, hq + 2 * ng_blocks + g)),
            pl.BlockSpec((None, tb, w), lambda bi, g, t: (bi, t, hz + g)),
            pl.BlockSpec((None, tb, LANES), lambda bi, g, t: (bi, t, 0)),
            pl.BlockSpec((GDN_CONV, w), lambda bi, g, t: (0, g)),
            pl.BlockSpec((GDN_CONV, w), lambda bi, g, t: (0, ng_blocks + g)),
            pl.BlockSpec((GDN_CONV, w), lambda bi, g, t: (0, 2 * ng_blocks + g)),
            pl.BlockSpec((1, LANES), lambda bi, g, t: (0, 0)),
            pl.BlockSpec((1, LANES), lambda bi, g, t: (0, 0)),
            pl.BlockSpec((1, LANES), lambda bi, g, t: (0, 0)),
        ],
        out_specs=pl.BlockSpec((None, tb, w), lambda bi, g, t: (bi, t, g)),
        out_shape=jax.ShapeDtypeStruct((b, s, nh * GDN_D), BF16),
        scratch_shapes=[
            pltpu.VMEM((hg, GDN_D, GDN_D), F32),
            pltpu.VMEM((3, SUBLANES, w), F32),
            pltpu.VMEM((SUBLANES + tb, w), F32),
        ],
        compiler_params=_cparams(("arbitrary", "arbitrary", "arbitrary")),
        name="gdn",
    )(proj, proj, proj, proj, small, conv_w, conv_w, conv_w, alog_v, dtb_v, ng)


DSA_QB = 128
DSA_KC = 512
DSA_ONES = 16
_KEY_NEG_INF = np.int32(np.array(0xFF800000, np.uint32).view(np.int32) ^ np.int32(0x7FFFFFFF))
_KEY_POS_INF_P1 = np.int32(0x7F800001)


def _key_to_f32(key):
    bits = key ^ ((key >> 31) & jnp.int32(0x7FFFFFFF))
    return lax.bitcast_convert_type(bits, F32)


def _tree_sum(xs):
    xs = list(xs)
    while len(xs) > 1:
        nxt = [xs[i] + xs[i + 1] for i in range(0, len(xs) - 1, 2)]
        if len(xs) % 2:
            nxt.append(xs[-1])
        xs = nxt
    return xs[0]


def _count_rows(hit):
    rows, n = hit.shape
    parts = hit.reshape(rows // SUBLANES, SUBLANES, n)
    return _tree_sum([parts[i] for i in range(rows // SUBLANES)])


def _dsa_body(q_ref, iq_ref, smq_ref, k_ref, v_ref, smk_ref, ikg_ref, o_ref,
              ikn_ref, vt_ref, s_ref, thrkey_ref, cnt_ref, jsel_ref, m_ref, alpha_ref, acc_ref, lg_ref, *, seq, k_top):
    qb = pl.program_id(1)
    kc = DSA_KC
    nq = DSA_QB
    rep = DSA_HEADS // DSA_KV_HEADS

    @pl.when(qb == 0)
    def _():
        def prep(i, carry):
            rs = pl.ds(pl.multiple_of(i * kc, kc), kc)
            sm = smk_ref[rs, :]
            lane = lax.broadcasted_iota(I32, sm.shape, 1)
            isk = lane < IDX_DIM
            mean = jnp.sum(jnp.where(isk, sm, 0.0), axis=-1, keepdims=True) * (1.0 / IDX_DIM)
            xc = jnp.where(isk, sm - mean, 0.0)
            var = jnp.sum(xc * xc, axis=-1, keepdims=True) * (1.0 / IDX_DIM)
            y = xc * lax.rsqrt(var + EPS) * ikg_ref[...]
            ikn_ref[0, rs, :] = y.astype(BF16)
            ikn_ref[1, rs, :] = pltpu.roll(y, IDX_DIM, axis=1).astype(BF16)
            vv = v_ref[rs, :].astype(F32)
            for g in range(DSA_KV_HEADS):
                vt_ref[g, i, 0:DSA_DH, :] = vv[:, g * DSA_DH:(g + 1) * DSA_DH].T.astype(BF16)
                vt_ref[g, i, DSA_DH:DSA_DH + DSA_ONES, :] = jnp.ones((DSA_ONES, kc), BF16)
            return carry
        lax.fori_loop(0, seq // kc, prep, 0)

    q_lo = qb * nq
    n_kc = (q_lo + nq + kc - 1) // kc
    qpos = q_lo + lax.broadcasted_iota(I32, (1, nq), 1)

    iq = iq_ref[...]
    n_pair = IDX_HEADS // 2
    iq_t = jnp.concatenate([iq[:, p * LANES:(p + 1) * LANES].astype(F32).T for p in range(n_pair)],
                           axis=1).astype(BF16)
    iw_t = smq_ref[...].T * ((IDX_HEADS ** -0.5) * (IDX_DIM ** -0.5))

    def score_chunk(i, carry):
        rs = pl.ds(pl.multiple_of(i * kc, kc), kc)
        lo = _dot(ikn_ref[0, rs, :], iq_t)
        hi = _dot(ikn_ref[1, rs, :], iq_t)
        sc = jnp.zeros((kc, nq), F32)
        for p in range(n_pair):
            cs = slice(p * nq, (p + 1) * nq)
            sc = sc + iw_t[SM_IW + 2 * p:SM_IW + 2 * p + 1, :] * jnp.maximum(lo[:, cs], 0.0)
            sc = sc + iw_t[SM_IW + 2 * p + 1:SM_IW + 2 * p + 2, :] * jnp.maximum(hi[:, cs], 0.0)
        kpos = i * kc + lax.broadcasted_iota(I32, (kc, nq), 0)
        s_ref[rs, :] = jnp.where(kpos <= qpos, sc + 0.0, -jnp.inf)
        return carry
    lax.fori_loop(0, n_kc, score_chunk, 0)

    kf = jnp.float32(k_top)

    def midpoint(lo, hi):
        return (lo >> 1) + (hi >> 1) + (lo & hi & 1)

    def bisect_static(rows):
        pieces = [slice(r0, min(r0 + kc, rows)) for r0 in range(0, rows, kc)]

        def count(hit_fn):
            parts = [_count_rows(jnp.where(hit_fn(s_ref[rs, :]), 1.0, 0.0)) for rs in pieces]
            return jnp.sum(_tree_sum(parts), axis=0, keepdims=True)

        def count_ge(thr):
            return count(lambda s: s >= thr)

        def bisect(_, carry):
            lo, hi = carry
            mid = midpoint(lo, hi)
            ok = count_ge(_key_to_f32(mid)) >= kf
            return jnp.where(ok, mid, lo), jnp.where(ok, hi, mid)
        lo0 = jnp.full((1, nq), _KEY_NEG_INF, I32)
        hi0 = jnp.full((1, nq), _KEY_POS_INF_P1, I32)
        lo, _ = lax.fori_loop(0, 32, bisect, (lo0, hi0))
        thrkey_ref[...] = lo
        thr = _key_to_f32(lo)
        cnt_ref[0] = count(lambda s: s > thr)
        cnt_ref[1] = count_ge(thr)

    for j in range(seq // nq):
        pl.when(qb == j)(functools.partial(bisect_static, (j + 1) * nq))
    thr = _key_to_f32(thrkey_ref[...])

    n_gt = cnt_ref[0]
    n_ge = cnt_ref[1]
    need = kf - n_gt
    jsel_ref[...] = jnp.full((1, nq), seq, I32)

    @pl.when(jnp.max(n_ge) > kf)
    def _():
        def count_eq_upto(jmax):
            def body(i, cnt):
                rs = pl.ds(pl.multiple_of(i * kc, kc), kc)
                kpos = i * kc + lax.broadcasted_iota(I32, (kc, nq), 0)
                hit = jnp.where(s_ref[rs, :] == thr, jnp.where(kpos <= jmax, 1.0, 0.0), 0.0)
                return cnt + _count_rows(hit)
            cnt = lax.fori_loop(0, n_kc, body, jnp.zeros((SUBLANES, nq), F32))
            return jnp.sum(cnt, axis=0, keepdims=True)

        def jbisect(_, carry):
            lo_j, hi_j = carry
            mid = (lo_j + hi_j) >> 1
            ok = count_eq_upto(mid) >= need
            return jnp.where(ok, lo_j, mid), jnp.where(ok, mid, hi_j)
        n_iter = int(np.ceil(np.log2(seq))) + 1
        lo_j0 = jnp.full((1, nq), -1, I32)
        hi_j0 = jnp.full((1, nq), seq - 1, I32)
        _, hi_j = lax.fori_loop(0, n_iter, jbisect, (lo_j0, hi_j0))
        jsel_ref[...] = hi_j

    jsel = jsel_ref[...]

    def selection_bias(i):
        s = s_ref[pl.ds(pl.multiple_of(i * kc, kc), kc), :]
        kpos = i * kc + lax.broadcasted_iota(I32, (kc, nq), 0)
        tie = jnp.where(s == thr, jnp.where(kpos <= jsel, 0.0, NEG_BIG), NEG_BIG)
        sel = jnp.where(s > thr, 0.0, tie)
        return jnp.where(kpos <= qpos, sel, NEG_BIG)

    qall = q_ref[...].astype(F32) * (DSA_DH ** -0.5 * np.log2(np.e))
    q_t = [jnp.concatenate([qall[:, (g * rep + r) * DSA_DH:(g * rep + r + 1) * DSA_DH].T
                            for r in range(rep)], axis=1).astype(BF16) for g in range(DSA_KV_HEADS)]
    m_ref[...] = jnp.full(m_ref.shape, NEG_BIG, F32)
    acc_ref[...] = jnp.zeros(acc_ref.shape, F32)

    def logits_phase(i):
        rs = pl.ds(pl.multiple_of(i * kc, kc), kc)
        bias_w = jnp.concatenate([selection_bias(i)] * rep, axis=1)
        for g in range(DSA_KV_HEADS):
            logit = _dot(k_ref[rs, g * DSA_DH:(g + 1) * DSA_DH], q_t[g]) + bias_w
            lg_ref[g] = logit
            m_old = m_ref[g]
            m_new = jnp.maximum(m_old, jnp.max(logit, axis=0, keepdims=True))
            alpha_ref[g] = jnp.exp2(m_old - m_new)
            m_ref[g] = m_new

    def value_phase(i):
        for g in range(DSA_KV_HEADS):
            p = jnp.exp2(lg_ref[g] - m_ref[g]).astype(BF16)
            acc_ref[g] = acc_ref[g] * alpha_ref[g] + _dot(vt_ref[g, i], p)

    logits_phase(0)

    def attn_chunk(i, carry):
        value_phase(i - 1)
        logits_phase(i)
        return carry
    lax.fori_loop(1, n_kc, attn_chunk, 0)
    value_phase(n_kc - 1)

    for g in range(DSA_KV_HEADS):
        acc = acc_ref[g]
        ot = acc[0:DSA_DH] / acc[DSA_DH:DSA_DH + 1]
        for r in range(rep):
            hh = g * rep + r
            o_ref[:, hh * DSA_DH:(hh + 1) * DSA_DH] = ot[:, r * nq:(r + 1) * nq].T.astype(o_ref.dtype)


def _dsa(dq, iq, dk, dv, small, ikg):
    b, s, _ = dq.shape
    nq = DSA_QB
    k_top = min(TOPK_MAX, s // 4)
    rep = DSA_HEADS // DSA_KV_HEADS
    kvw = DSA_KV_HEADS * DSA_DH
    return pl.pallas_call(
        functools.partial(_dsa_body, seq=s, k_top=k_top),
        grid=(b, s // nq),
        in_specs=[
            pl.BlockSpec((None, nq, DSA_HEADS * DSA_DH), lambda bi, qb: (bi, qb, 0)),
            pl.BlockSpec((None, nq, IDX_HEADS * IDX_DIM), lambda bi, qb: (bi, qb, 0)),
            pl.BlockSpec((None, nq, LANES), lambda bi, qb: (bi, qb, 0)),
            pl.BlockSpec((None, s, kvw), lambda bi, qb: (bi, 0, 0)),
            pl.BlockSpec((None, s, kvw), lambda bi, qb: (bi, 0, 0)),
            pl.BlockSpec((None, s, LANES), lambda bi, qb: (bi, 0, 0)),
            pl.BlockSpec((1, LANES), lambda bi, qb: (0, 0)),
        ],
        out_specs=pl.BlockSpec((None, nq, DSA_HEADS * DSA_DH), lambda bi, qb: (bi, qb, 0)),
        out_shape=jax.ShapeDtypeStruct((b, s, DSA_HEADS * DSA_DH), BF16),
        scratch_shapes=[
            pltpu.VMEM((2, s, LANES), BF16),
            pltpu.VMEM((DSA_KV_HEADS, s // DSA_KC, DSA_DH + DSA_ONES, DSA_KC), BF16),
            pltpu.VMEM((s, nq), F32),
            pltpu.VMEM((1, nq), I32),
            pltpu.VMEM((2, 1, nq), F32),
            pltpu.VMEM((1, nq), I32),
            pltpu.VMEM((DSA_KV_HEADS, 1, rep * nq), F32),
            pltpu.VMEM((DSA_KV_HEADS, 1, rep * nq), F32),
            pltpu.VMEM((DSA_KV_HEADS, DSA_DH + DSA_ONES, rep * nq), F32),
            pltpu.VMEM((DSA_KV_HEADS, DSA_KC, rep * nq), F32),
        ],
        compiler_params=_cparams(("arbitrary", "arbitrary")),
        name="dsa",
    )(dq, iq, small, dk, dv, small, ikg)


FFN_CHUNK = 256


def _mix_ffn_body(x_ref, ga_ref, gb_ref, ba_ref, bb_ref, og_ref, od_ref, wo_ref,
                  g2_ref, wup_ref, cw_ref, cb_ref, wdn_ref, gf_ref, o_ref,
                  carry_ref, xx_ref, perm_ref, *, tm):
    t = pl.program_id(1)

    @pl.when(t == 0)
    def _():
        carry_ref[...] = jnp.zeros_like(carry_ref)

    sa = _sigmoid(ga_ref[...].astype(F32) + ba_ref[...])
    sb = _sigmoid(gb_ref[...].astype(F32) + bb_ref[...])
    mixed = sa * og_ref[...].astype(F32) + sb * od_ref[...].astype(F32)
    h = x_ref[...] + _dot(mixed.astype(BF16), wo_ref[...])
    hn = ((h * lax.rsqrt(jnp.mean(h * h, axis=-1, keepdims=True) + EPS)) * g2_ref[...]).astype(BF16)

    def up_cols(c0):
        cs = slice(c0, c0 + FFN_CHUNK)
        up = _dot(hn, wup_ref[:, cs])
        for k in range(FFN_CHUNK // LANES):
            blk = c0 // LANES + k
            ls = slice(k * LANES, (k + 1) * LANES)
            xx_ref[blk, 0:SUBLANES, :] = carry_ref[:, c0 + k * LANES:c0 + (k + 1) * LANES]
            xx_ref[blk, SUBLANES:SUBLANES + tm, :] = up[:, ls]
        carry_ref[:, cs] = up[tm - SUBLANES:tm, :]

    def conv_cols(c0):
        cols = []
        for k in range(FFN_CHUNK // LANES):
            blk = c0 // LANES + k
            ls = slice(c0 + k * LANES, c0 + (k + 1) * LANES)
            w = cw_ref[:, ls]
            groups = []
            for r in range(SUBLANES):
                y = cb_ref[:, ls]
                for j in range(FFN_CONV):
                    s0 = SUBLANES - (FFN_CONV - 1) + j + r
                    y = y + w[j:j + 1, :] * xx_ref[blk, pl.ds(s0, tm // SUBLANES, stride=SUBLANES), :]
                groups.append(y)
            cols.append(jnp.concatenate(groups, axis=0))
        return jnp.concatenate(cols, axis=1)

    n_chunk = D_FF // FFN_CHUNK
    up_cols(0)
    up_cols(D_FF)
    acc = jnp.zeros((tm, D_MODEL), F32)
    for c in range(n_chunk):
        if c + 1 < n_chunk:
            up_cols((c + 1) * FFN_CHUNK)
            up_cols(D_FF + (c + 1) * FFN_CHUNK)
        gate = conv_cols(c * FFN_CHUNK)
        val = conv_cols(D_FF + c * FFN_CHUNK)
        act = (_silu(gate) * val).astype(BF16)
        acc = acc + _dot(act, wdn_ref[c * FFN_CHUNK:(c + 1) * FFN_CHUNK, :])
    g_rows = tm // SUBLANES
    for k in range(D_MODEL // LANES):
        for r in range(SUBLANES):
            perm_ref[k, pl.ds(r, g_rows, stride=SUBLANES), :] = acc[r * g_rows:(r + 1) * g_rows,
                                                                    k * LANES:(k + 1) * LANES]
    h2 = h + jnp.concatenate([perm_ref[k] for k in range(D_MODEL // LANES)], axis=1)
    o_ref[...] = (h2 * lax.rsqrt(jnp.mean(h2 * h2, axis=-1, keepdims=True) + EPS)) * gf_ref[...]


def _mix_ffn(x, gates, bias_a, bias_b, o_gdn, o_dsa, w_out, g2, w_up, conv_w, conv_b, w_down, gf, *, tm=256):
    b, s, _ = x.shape
    const = lambda bi, t: (0, 0)
    row = lambda bi, t: (bi, t, 0)
    return pl.pallas_call(
        functools.partial(_mix_ffn_body, tm=tm),
        grid=(b, s // tm),
        in_specs=[
            pl.BlockSpec((None, tm, D_MODEL), row),
            pl.BlockSpec((None, tm, D_MODEL), lambda bi, t: (bi, t, 0)),
            pl.BlockSpec((None, tm, D_MODEL), lambda bi, t: (bi, t, (COL_GB - COL_GA) // D_MODEL)),
            pl.BlockSpec((1, D_MODEL), const),
            pl.BlockSpec((1, D_MODEL), const),
            pl.BlockSpec((None, tm, D_MODEL), row),
            pl.BlockSpec((None, tm, D_MODEL), row),
            pl.BlockSpec((D_MODEL, D_MODEL), const, pipeline_mode=pl.Buffered(1)),
            pl.BlockSpec((1, D_MODEL), const),
            pl.BlockSpec((D_MODEL, 2 * D_FF), const, pipeline_mode=pl.Buffered(1)),
            pl.BlockSpec((FFN_CONV, 2 * D_FF), const),
            pl.BlockSpec((1, 2 * D_FF), const),
            pl.BlockSpec((D_FF, D_MODEL), const, pipeline_mode=pl.Buffered(1)),
            pl.BlockSpec((1, D_MODEL), const),
        ],
        out_specs=pl.BlockSpec((None, tm, D_MODEL), lambda bi, t: (bi, t, 0)),
        out_shape=jax.ShapeDtypeStruct((b, s, D_MODEL), F32),
        scratch_shapes=[
            pltpu.VMEM((SUBLANES, 2 * D_FF), F32),
            pltpu.VMEM((2 * D_FF // LANES, SUBLANES + tm, LANES), F32),
            pltpu.VMEM((D_MODEL // LANES, tm, LANES), F32),
        ],
        compiler_params=_cparams(("arbitrary", "arbitrary")),
        name="mix_ffn",
    )(x, gates, gates, bias_a, bias_b, o_gdn, o_dsa, w_out, g2, w_up, conv_w, conv_b, w_down, gf)


def _rearranged_w_in(w):
    o = np.cumsum([0, 3072, 1024, 8, 8, 1024, 256, 256, 512, 64, 8, 1024, 1024])
    g_qkv, g_z, g_a, g_b, d_q, d_k, d_v, i_q, i_k, i_w, gate_a, gate_b = (
        w[:, int(o[i]):int(o[i + 1])] for i in range(12))
    big = jnp.concatenate([g_qkv, g_z, d_q, d_k, d_v, i_q, gate_a, gate_b], axis=1).astype(BF16)
    pad = jnp.zeros((w.shape[0], LANES - (IDX_DIM + 3 * 8)), w.dtype)
    small = jnp.concatenate([i_k, g_a, g_b, i_w, pad], axis=1).astype(BF16)
    return big, small


def _lane_vec(v, offset):
    return jnp.zeros((1, LANES), F32).at[0, offset:offset + v.shape[0]].set(v.astype(F32))


def kernel(x, norm1_g, w_in, gdn_conv_w, gdn_a_log, gdn_dt_bias, gdn_norm_g, idx_k_norm_g,
           branch_gate_b, w_out, norm2_g, ffn_w_up, ffn_conv_w, ffn_conv_b, ffn_w_down, final_g):
    b, s, d = x.shape
    depth = norm1_g.shape[0]
    assert depth == 1, "the final RMSNorm is fused into the (single) channel-mixer call"
    h = x
    for l in range(depth):
        w_big, w_small = _rearranged_w_in(w_in[l])
        outs = _inproj(h.reshape(b * s, d), norm1_g[l][None, :], w_big, w_small)
        gdn_in, dq, dk, dv, iq, gates, small = (o.reshape(b, s, -1) for o in outs)
        o_gdn = _gdn(gdn_in, small, gdn_conv_w[l], _lane_vec(gdn_a_log[l], SM_A),
                     _lane_vec(gdn_dt_bias[l], SM_A), gdn_norm_g[l][None, :].astype(F32))
        o_dsa = _dsa(dq, iq, dk, dv, small, _lane_vec(idx_k_norm_g[l], SM_IK))
        gb = branch_gate_b[l]
        h = _mix_ffn(h, gates, gb[None, :D_MODEL], gb[None, D_MODEL:], o_gdn, o_dsa, w_out[l].astype(BF16),
                     norm2_g[l][None, :], ffn_w_up[l].astype(BF16), ffn_conv_w[l],
                     ffn_conv_b[l][None, :], ffn_w_down[l].astype(BF16), final_g[None, :])
    return h
```
